```python
import math
import jax, jax.numpy as jnp
from jax import lax
import numpy as np

D_MODEL = 1024
BATCH = 4
SEQ = 8192
DEPTH = 1

MEM_LEN = 256
MIX_WIDTH = D_MODEL
ATT_WIDTH = MIX_WIDTH // 2
CONV_WIDTH = MIX_WIDTH - ATT_WIDTH
ATT_HEADS = 4
ATT_HEAD_DIM = ATT_WIDTH // ATT_HEADS
QK_DIM = ATT_HEAD_DIM // 2
QK_TOTAL = ATT_HEADS * 2 * QK_DIM
CONV_GROUPS = 4
CONV_GROUP_DIM = CONV_WIDTH // CONV_GROUPS
CONV_KERNEL = 31
IN_COLS = 2 * QK_TOTAL + ATT_WIDTH + 2 * CONV_WIDTH
MEM_HEADS = 4
MEM_HEAD_DIM = D_MODEL // MEM_HEADS
D_FF = -(-8 * D_MODEL // (3 * 256)) * 256
Q_BLOCK = 128
LN_EPS = 1e-5
DEEPNORM_ALPHA = (2 * DEPTH) ** 0.25
DEEPNORM_BETA = (8 * DEPTH) ** -0.25

kernel_name = "hybrid_diffattn_conformer_deepnorm"


def layer_norm(x, g, b):
    xf = x.astype(jnp.float32)
    mu = xf.mean(-1, keepdims=True)
    var = jnp.square(xf - mu).mean(-1, keepdims=True)
    return ((xf - mu) * lax.rsqrt(var + LN_EPS) * g.astype(jnp.float32) + b.astype(jnp.float32)).astype(x.dtype)


def rms_norm(x, g):
    xf = x.astype(jnp.float32)
    return xf * lax.rsqrt(jnp.mean(xf * xf, -1, keepdims=True) + LN_EPS) * g.astype(jnp.float32)


def alibi_slopes(n_heads):
    return jnp.exp2(-8.0 * jnp.arange(1, n_heads + 1, dtype=jnp.float32) / n_heads)


def diff_attention(q, k, v, lam, slopes):
    b, s, h, _, dk = q.shape
    nb = s // Q_BLOCK
    qb = q.reshape(b, nb, Q_BLOCK, h, 2, dk).transpose(1, 0, 2, 3, 4, 5)
    vf = v.astype(jnp.float32)
    k_pos = jnp.arange(s)
    scale = dk ** -0.5

    def block(args):
        q_blk, i = args
        q_pos = i * Q_BLOCK + jnp.arange(Q_BLOCK)
        dist = q_pos[:, None] - k_pos[None, :]
        bias = -slopes[:, None, None] * dist.astype(jnp.float32)
        logits = jnp.einsum('bqhcd,bkhcd->bhcqk', q_blk, k,
                            preferred_element_type=jnp.float32) * scale + bias[None, :, None]
        logits = jnp.where(dist[None, None, None] >= 0, logits, -jnp.inf)
        p = jax.nn.softmax(logits, axis=-1)
        a = p[:, :, 0] - lam * p[:, :, 1]
        return jnp.einsum('bhqk,bkhe->bqhe', a, vf)

    out = lax.map(block, (qb, jnp.arange(nb)))
    return out.transpose(1, 0, 2, 3, 4).reshape(b, s, h, -1)


def causal_depthwise_conv(u, w, bias):
    c = u.shape[-1]
    out = lax.conv_general_dilated(u, w[:, None, :].astype(u.dtype), window_strides=(1,),
                                   padding=((w.shape[0] - 1, 0),),
                                   dimension_numbers=('NWC', 'WIO', 'NWC'),
                                   feature_group_count=c)
    return out + bias


def setup_inputs(seed: int = 0) -> dict:
    key = jax.random.key(seed)
    ks = iter(jax.random.split(key, 40))

    def nrm(shape, scale):
        return jax.random.normal(next(ks), shape, jnp.float32) * scale

    def gain(shape):
        return 1.0 + nrm(shape, 0.02)

    L = DEPTH
    d = {}
    d["x"] = nrm((BATCH, SEQ, D_MODEL), 1.0)
    d["mem"] = nrm((BATCH, MEM_LEN, D_MODEL), 1.0)
    d["in_norm_g"] = gain((D_MODEL,))
    d["in_norm_b"] = nrm((D_MODEL,), 0.02)
    d["w_in"] = nrm((L, D_MODEL, IN_COLS), D_MODEL ** -0.5)
    d["lambda_q1"] = nrm((L, QK_DIM), 0.1)
    d["lambda_k1"] = nrm((L, QK_DIM), 0.1)
    d["lambda_q2"] = nrm((L, QK_DIM), 0.1)
    d["lambda_k2"] = nrm((L, QK_DIM), 0.1)
    d["subln_g"] = gain((L, ATT_HEAD_DIM))
    d["conv_w"] = nrm((L, CONV_KERNEL, CONV_WIDTH), CONV_KERNEL ** -0.5)
    d["conv_b"] = nrm((L, CONV_WIDTH), 0.02)
    d["conv_norm_g"] = gain((L, CONV_WIDTH))
    d["conv_norm_b"] = nrm((L, CONV_WIDTH), 0.02)
    d["w_pw"] = nrm((L, CONV_WIDTH, CONV_WIDTH), CONV_WIDTH ** -0.5)
    d["b_pw"] = nrm((L, CONV_WIDTH), 0.02)
    d["w_o"] = nrm((L, MIX_WIDTH, D_MODEL), DEEPNORM_BETA * MIX_WIDTH ** -0.5)
    d["ln1_g"] = gain((L, D_MODEL))
    d["ln1_b"] = nrm((L, D_MODEL), 0.02)
    d["w_q_mem"] = nrm((L, D_MODEL, D_MODEL), D_MODEL ** -0.5)
    d["w_kv_mem"] = nrm((L, D_MODEL, 2 * D_MODEL), D_MODEL ** -0.5)
    d["w_o_mem"] = nrm((L, D_MODEL, D_MODEL), DEEPNORM_BETA * D_MODEL ** -0.5)
    d["ln2_g"] = gain((L, D_MODEL))
    d["ln2_b"] = nrm((L, D_MODEL), 0.02)
    d["w_gate"] = nrm((L, D_MODEL, D_FF), D_MODEL ** -0.5)
    d["w_up"] = nrm((L, D_MODEL, D_FF), D_MODEL ** -0.5)
    d["w_down"] = nrm((L, D_FF, D_MODEL), DEEPNORM_BETA * D_FF ** -0.5)
    d["ln3_g"] = gain((L, D_MODEL))
    d["ln3_b"] = nrm((L, D_MODEL), 0.02)
    return d


def reference(x, mem, in_norm_g, in_norm_b, w_in, lambda_q1, lambda_k1, lambda_q2, lambda_k2,
              subln_g, conv_w, conv_b, conv_norm_g, conv_norm_b, w_pw, b_pw, w_o, ln1_g, ln1_b,
              w_q_mem, w_kv_mem, w_o_mem, ln2_g, ln2_b, w_gate, w_up, w_down, ln3_g, ln3_b):
    bsz, s, _ = x.shape
    slopes = alibi_slopes(ATT_HEADS)
    x = layer_norm(x, in_norm_g, in_norm_b)
    for l in range(DEPTH):
        lam_init = 0.8 - 0.6 * math.exp(-0.3 * l)
        proj = x @ w_in[l]
        q, k, v, c_val, c_gate = jnp.split(
            proj, [QK_TOTAL, 2 * QK_TOTAL, 2 * QK_TOTAL + ATT_WIDTH,
                   2 * QK_TOTAL + ATT_WIDTH + CONV_WIDTH], axis=-1)
        q = q.reshape(bsz, s, ATT_HEADS, 2, QK_DIM)
        k = k.reshape(bsz, s, ATT_HEADS, 2, QK_DIM)
        v = v.reshape(bsz, s, ATT_HEADS, ATT_HEAD_DIM)
        lam = (jnp.exp(jnp.sum(lambda_q1[l] * lambda_k1[l]).astype(jnp.float32))
               - jnp.exp(jnp.sum(lambda_q2[l] * lambda_k2[l]).astype(jnp.float32)) + lam_init)
        att = diff_attention(q, k, v, lam, slopes)
        att = rms_norm(att, subln_g[l]) * (1.0 - lam_init)
        att = att.reshape(bsz, s, ATT_WIDTH).astype(x.dtype)
        glu = c_val * jax.nn.sigmoid(c_gate)
        c = causal_depthwise_conv(glu, conv_w[l], conv_b[l])
        c = layer_norm(c.reshape(bsz, s, CONV_GROUPS, CONV_GROUP_DIM),
                       conv_norm_g[l].reshape(CONV_GROUPS, CONV_GROUP_DIM),
                       conv_norm_b[l].reshape(CONV_GROUPS, CONV_GROUP_DIM)).reshape(bsz, s, CONV_WIDTH)
        c = jax.nn.silu(c) @ w_pw[l] + b_pw[l]
        mix = jnp.concatenate([att, c], axis=-1) @ w_o[l]
        x = layer_norm(DEEPNORM_ALPHA * x + mix, ln1_g[l], ln1_b[l])
        qc = (x @ w_q_mem[l]).reshape(bsz, s, MEM_HEADS, MEM_HEAD_DIM)
        kc, vc = jnp.split(mem @ w_kv_mem[l], 2, axis=-1)
        kc = kc.reshape(bsz, -1, MEM_HEADS, MEM_HEAD_DIM)
        vc = vc.reshape(bsz, -1, MEM_HEADS, MEM_HEAD_DIM)
        logits = jnp.einsum('bshd,bmhd->bhsm', qc, kc,
                            preferred_element_type=jnp.float32) * (MEM_HEAD_DIM ** -0.5)
        p = jax.nn.softmax(logits, axis=-1)
        ca = jnp.einsum('bhsm,bmhd->bshd', p, vc.astype(jnp.float32)).reshape(bsz, s, D_MODEL).astype(x.dtype)
        x = layer_norm(DEEPNORM_ALPHA * x + ca @ w_o_mem[l], ln2_g[l], ln2_b[l])
        hdn = jax.nn.silu(x @ w_gate[l]) * (x @ w_up[l])
        x = layer_norm(DEEPNORM_ALPHA * x + hdn @ w_down[l], ln3_g[l], ln3_b[l])
    return x
```

```python
import functools
import math

import jax
import jax.numpy as jnp
from jax import lax
from jax.experimental import pallas as pl
from jax.experimental.pallas import tpu as pltpu

D_MODEL = 1024
DEPTH = 1
MEM_LEN = 256
ATT_WIDTH = 512
CONV_WIDTH = 512
ATT_HEADS = 4
ATT_HEAD_DIM = 128
QK_DIM = 64
QK_TOTAL = 512
CONV_GROUPS = 4
CONV_GROUP_DIM = 128
CONV_KERNEL = 31
MEM_HEADS = 4
MEM_HEAD_DIM = 256
D_FF = 2816
LN_EPS = 1e-5
DEEPNORM_ALPHA = (2 * DEPTH) ** 0.25

LOG2E = 1.4426950408889634
NEG_BIG = -1e30

PROJ_TM = 512
ATT_BLK = 256
CONV_TS = 256
CONV_HALO = 32
CONV_ROWS = 64
POST_TM = 512
FF_CHUNK = 1408

VMEM_LIMIT = 60 * 1024 * 1024

_NT = (((1,), (1,)), ((), ()))


def _layer_norm(x, g, b):
    mu = jnp.mean(x, axis=-1, keepdims=True)
    xc = x - mu
    var = jnp.mean(xc * xc, axis=-1, keepdims=True)
    return xc * lax.rsqrt(var + LN_EPS) * g + b


def _dot(a, b):
    return jnp.dot(a, b, preferred_element_type=jnp.float32)


def _dot_nt(a, b):
    return lax.dot_general(a, b, _NT, preferred_element_type=jnp.float32)


def _inproj_kernel(x_ref, g_ref, b_ref, wqT_ref, wk_ref, wvT_ref, wc_ref,
                   xn_ref, qT_ref, k_ref, vT_ref, glu_ref):
    xn = _layer_norm(x_ref[0], g_ref[...], b_ref[...])
    xn_ref[0] = xn
    xb = xn.astype(jnp.bfloat16)
    k_ref[0] = _dot(xb, wk_ref[...]).astype(jnp.bfloat16)
    qscale = (QK_DIM ** -0.5) * LOG2E
    qT = _dot_nt(wqT_ref[...], xb) * qscale
    vT = _dot_nt(wvT_ref[...], xb)
    for j in range(PROJ_TM // ATT_BLK):
        sl = slice(j * ATT_BLK, (j + 1) * ATT_BLK)
        qT_ref[0, j] = qT[:, sl].astype(jnp.bfloat16)
        vT_ref[0, j] = vT[:, sl].astype(jnp.bfloat16)
    c_val = _dot(xb, wc_ref[:, :CONV_WIDTH])
    c_gate = _dot(xb, wc_ref[:, CONV_WIDTH:])
    glu_ref[0] = c_val * (1.0 / (1.0 + jnp.exp(-c_gate)))


def _in_proj(x, g, b, wqT, wk, wvT, wc):
    bsz, s, d = x.shape
    nt = s // PROJ_TM
    nb = PROJ_TM // ATT_BLK
    const = lambda bi, si: (0, 0)
    return pl.pallas_call(
        _inproj_kernel,
        grid=(bsz, nt),
        in_specs=[
            pl.BlockSpec((1, PROJ_TM, d), lambda bi, si: (bi, si, 0)),
            pl.BlockSpec((1, d), const),
            pl.BlockSpec((1, d), const),
            pl.BlockSpec((QK_TOTAL, d), const),
            pl.BlockSpec((d, QK_TOTAL), const),
            pl.BlockSpec((ATT_WIDTH, d), const),
            pl.BlockSpec((d, 2 * CONV_WIDTH), const),
        ],
        out_specs=[
            pl.BlockSpec((1, PROJ_TM, d), lambda bi, si: (bi, si, 0)),
            pl.BlockSpec((1, nb, QK_TOTAL, ATT_BLK), lambda bi, si: (bi, si, 0, 0)),
            pl.BlockSpec((1, PROJ_TM, QK_TOTAL), lambda bi, si: (bi, si, 0)),
            pl.BlockSpec((1, nb, ATT_WIDTH, ATT_BLK), lambda bi, si: (bi, si, 0, 0)),
            pl.BlockSpec((1, PROJ_TM, CONV_WIDTH), lambda bi, si: (bi, si, 0)),
        ],
        out_shape=[
            jax.ShapeDtypeStruct((bsz, s, d), jnp.float32),
            jax.ShapeDtypeStruct((bsz, s // ATT_BLK, QK_TOTAL, ATT_BLK), jnp.bfloat16),
            jax.ShapeDtypeStruct((bsz, s, QK_TOTAL), jnp.bfloat16),
            jax.ShapeDtypeStruct((bsz, s // ATT_BLK, ATT_WIDTH, ATT_BLK), jnp.bfloat16),
            jax.ShapeDtypeStruct((bsz, s, CONV_WIDTH), jnp.float32),
        ],
        compiler_params=pltpu.CompilerParams(
            dimension_semantics=("arbitrary", "arbitrary"),
            vmem_limit_bytes=VMEM_LIMIT),
        name="in_proj",
    )(x, g, b, wqT, wk, wvT, wc)


def _attn_kernel(slope_ref, qT_ref, k_ref, vT_ref, lq1_ref, lk1_ref, lq2_ref, lk2_ref,
                 gain_ref, o_ref, qs_ref, base_ref, m_ref, l_ref, acc_ref, *, lam_init):
    blk = ATT_BLK
    h = pl.program_id(1)
    qi = pl.program_id(2)
    slope2 = slope_ref[h]

    qT = qT_ref[0, 0]
    row = lax.broadcasted_iota(jnp.int32, (ATT_HEAD_DIM, blk), 0)
    zero = jnp.zeros_like(qT)
    qs_ref[:, :blk] = jnp.where(row < QK_DIM, qT, zero)
    qs_ref[:, blk:] = jnp.where(row >= QK_DIM, qT, zero)

    kr = lax.broadcasted_iota(jnp.int32, (blk, blk), 0)
    qc = lax.broadcasted_iota(jnp.int32, (blk, blk), 1)
    base = slope2 * (kr - qc).astype(jnp.float32)
    base_ref[:, :blk] = base
    base_ref[:, blk:] = base

    m_ref[...] = jnp.full(m_ref.shape, NEG_BIG, jnp.float32)
    l_ref[...] = jnp.zeros(l_ref.shape, jnp.float32)
    acc_ref[...] = jnp.zeros(acc_ref.shape, jnp.float32)

    def step(j, diagonal):
        start = pl.multiple_of(j * blk, blk)
        kb = k_ref[0, pl.ds(start, blk), :]
        vTb = vT_ref[0, j]
        s = _dot(kb, qs_ref[...]) + base_ref[...]
        if diagonal:
            valid = jnp.concatenate([kr <= qc, kr <= qc], axis=1)
            s = jnp.where(valid, s, NEG_BIG)
        off = slope2 * ((j - qi) * blk).astype(jnp.float32)
        m_old = m_ref[...]
        m_new = jnp.maximum(m_old, jnp.max(s, axis=0, keepdims=True) + off)
        p = jnp.exp2(s - (m_new - off))
        alpha = jnp.exp2(m_old - m_new)
        l_ref[...] = alpha * l_ref[...] + jnp.sum(p, axis=0, keepdims=True)
        acc_ref[...] = alpha * acc_ref[...] + _dot(vTb, p.astype(jnp.bfloat16))
        m_ref[...] = m_new

    def body(j, carry):
        step(j, False)
        return carry

    lax.fori_loop(0, qi, body, 0)
    step(qi, True)

    lam = (jnp.exp(jnp.sum(lq1_ref[...] * lk1_ref[...], axis=-1, keepdims=True))
           - jnp.exp(jnp.sum(lq2_ref[...] * lk2_ref[...], axis=-1, keepdims=True))
           + lam_init)
    acc = acc_ref[...]
    l = l_ref[...]
    oT = acc[:, :blk] / l[:, :blk] - lam * (acc[:, blk:] / l[:, blk:])
    ms = jnp.mean(oT * oT, axis=0, keepdims=True)
    oT = oT * lax.rsqrt(ms + LN_EPS) * gain_ref[...] * (1.0 - lam_init)
    o_ref[0] = oT.T.astype(o_ref.dtype)


def _diff_attention(slopes2, qT, k, vT, lq1, lk1, lq2, lk2, gain_col, lam_init):
    bsz, nblk, _, blk = qT.shape
    s = nblk * blk
    vec = lambda bi, h, qi: (0, 0)
    return pl.pallas_call(
        functools.partial(_attn_kernel, lam_init=lam_init),
        grid=(bsz, ATT_HEADS, nblk),
        in_specs=[
            pl.BlockSpec(memory_space=pltpu.SMEM),
            pl.BlockSpec((1, 1, ATT_HEAD_DIM, blk), lambda bi, h, qi: (bi, qi, h, 0)),
            pl.BlockSpec((1, s, ATT_HEAD_DIM), lambda bi, h, qi: (bi, 0, h)),
            pl.BlockSpec((1, nblk, ATT_HEAD_DIM, blk), lambda bi, h, qi: (bi, 0, h, 0)),
            pl.BlockSpec((1, QK_DIM), vec),
            pl.BlockSpec((1, QK_DIM), vec),
            pl.BlockSpec((1, QK_DIM), vec),
            pl.BlockSpec((1, QK_DIM), vec),
            pl.BlockSpec((ATT_HEAD_DIM, 1), vec),
        ],
        out_specs=pl.BlockSpec((1, blk, ATT_HEAD_DIM), lambda bi, h, qi: (bi, qi, h)),
        out_shape=jax.ShapeDtypeStruct((bsz, s, ATT_WIDTH), jnp.bfloat16),
        scratch_shapes=[
            pltpu.VMEM((ATT_HEAD_DIM, 2 * blk), jnp.bfloat16),
            pltpu.VMEM((blk, 2 * blk), jnp.float32),
            pltpu.VMEM((1, 2 * blk), jnp.float32),
            pltpu.VMEM((1, 2 * blk), jnp.float32),
            pltpu.VMEM((ATT_HEAD_DIM, 2 * blk), jnp.float32),
        ],
        compiler_params=pltpu.CompilerParams(
            dimension_semantics=("arbitrary", "arbitrary", "arbitrary"),
            vmem_limit_bytes=VMEM_LIMIT),
        name="diff_attn",
    )(slopes2, qT, k, vT, lq1, lk1, lq2, lk2, gain_col)


def _conv_kernel(prev_ref, cur_ref, w_ref, cb_ref, ng_ref, nb_ref, wpw_ref, bpw_ref,
                 o_ref, xx_ref, act_ref):
    si = pl.program_id(1)
    prev = prev_ref[0]
    xx_ref[:CONV_HALO] = jnp.where(si > 0, prev, jnp.zeros_like(prev))
    xx_ref[CONV_HALO:] = cur_ref[0]
    first = CONV_HALO - (CONV_KERNEL - 1)
    for rc in range(CONV_TS // CONV_ROWS):
        r0 = rc * CONV_ROWS
        acc = jnp.broadcast_to(cb_ref[...], (CONV_ROWS, CONV_WIDTH))
        for j in range(CONV_KERNEL):
            acc = acc + w_ref[j:j + 1, :] * xx_ref[r0 + first + j:r0 + first + j + CONV_ROWS, :]
        for g in range(CONV_GROUPS):
            gs = slice(g * CONV_GROUP_DIM, (g + 1) * CONV_GROUP_DIM)
            y = _layer_norm(acc[:, gs], ng_ref[:, gs], nb_ref[:, gs])
            y = y * (1.0 / (1.0 + jnp.exp(-y)))
            act_ref[r0:r0 + CONV_ROWS, gs] = y.astype(jnp.bfloat16)
    o_ref[0] = (_dot(act_ref[...], wpw_ref[...]) + bpw_ref[...]).astype(o_ref.dtype)


def _conv_module(glu, w, cb, ng, nb, wpw, bpw):
    bsz, s, c = glu.shape
    nt = s // CONV_TS
    ratio = CONV_TS // CONV_HALO
    const = lambda bi, si: (0, 0)
    return pl.pallas_call(
        _conv_kernel,
        grid=(bsz, nt),
        in_specs=[
            pl.BlockSpec((1, CONV_HALO, c), lambda bi, si: (bi, jnp.maximum(si * ratio - 1, 0), 0)),
            pl.BlockSpec((1, CONV_TS, c), lambda bi, si: (bi, si, 0)),
            pl.BlockSpec((CONV_KERNEL, c), const),
            pl.BlockSpec((1, c), const),
            pl.BlockSpec((1, c), const),
            pl.BlockSpec((1, c), const),
            pl.BlockSpec((c, c), const),
            pl.BlockSpec((1, c), const),
        ],
        out_specs=pl.BlockSpec((1, CONV_TS, c), lambda bi, si: (bi, si, 0)),
        out_shape=jax.ShapeDtypeStruct((bsz, s, c), jnp.bfloat16),
        scratch_shapes=[
            pltpu.VMEM((CONV_HALO + CONV_TS, c), jnp.float32),
            pltpu.VMEM((CONV_TS, c), jnp.bfloat16),
        ],
        compiler_params=pltpu.CompilerParams(
            dimension_semantics=("arbitrary", "arbitrary"),
            vmem_limit_bytes=VMEM_LIMIT),
        name="conv_mod",
    )(glu, glu, w, cb, ng, nb, wpw, bpw)


def _memkv_kernel(mem_ref, w_ref, kc_ref, vc_ref):
    mb = mem_ref[0].astype(jnp.bfloat16)
    kc_ref[0] = _dot(mb, w_ref[:, :D_MODEL]).astype(jnp.bfloat16)
    vc_ref[0] = _dot(mb, w_ref[:, D_MODEL:]).astype(jnp.bfloat16)


def _mem_kv(mem, w_kv):
    bsz, m, d = mem.shape
    return pl.pallas_call(
        _memkv_kernel,
        grid=(bsz,),
        in_specs=[
            pl.BlockSpec((1, m, d), lambda bi: (bi, 0, 0)),
            pl.BlockSpec((d, 2 * d), lambda bi: (0, 0)),
        ],
        out_specs=[
            pl.BlockSpec((1, m, d), lambda bi: (bi, 0, 0)),
            pl.BlockSpec((1, m, d), lambda bi: (bi, 0, 0)),
        ],
        out_shape=[
            jax.ShapeDtypeStruct((bsz, m, d), jnp.bfloat16),
            jax.ShapeDtypeStruct((bsz, m, d), jnp.bfloat16),
        ],
        compiler_params=pltpu.CompilerParams(
            dimension_semantics=("arbitrary",), vmem_limit_bytes=VMEM_LIMIT),
        name="mem_kv",
    )(mem, w_kv)


def _post_kernel(xn_ref, att_ref, c_ref, kc_ref, vc_ref,
                 wo_ref, g1_ref, b1_ref, wq_ref, wom_ref, g2_ref, b2_ref,
                 wg_ref, wu_ref, wd_ref, g3_ref, b3_ref, o_ref, ca_ref):
    bf16 = jnp.bfloat16
    mix = _dot(att_ref[...], wo_ref[:ATT_WIDTH, :]) + _dot(c_ref[...], wo_ref[ATT_WIDTH:, :])
    x1 = _layer_norm(DEEPNORM_ALPHA * xn_ref[...] + mix, g1_ref[...], b1_ref[...])

    qscale = (MEM_HEAD_DIM ** -0.5) * LOG2E
    qm = (_dot(x1.astype(bf16), wq_ref[...]) * qscale).astype(bf16)
    for h in range(MEM_HEADS):
        hs = slice(h * MEM_HEAD_DIM, (h + 1) * MEM_HEAD_DIM)
        logits = _dot_nt(qm[:, hs], kc_ref[0, :, hs])
        p = jnp.exp2(logits - jnp.max(logits, axis=-1, keepdims=True))
        denom = jnp.sum(p, axis=-1, keepdims=True)
        ca = _dot(p.astype(bf16), vc_ref[0, :, hs]) / denom
        ca_ref[:, hs] = ca.astype(bf16)
    x2 = _layer_norm(DEEPNORM_ALPHA * x1 + _dot(ca_ref[...], wom_ref[...]),
                     g2_ref[...], b2_ref[...])

    x2b = x2.astype(bf16)
    down = None
    for c in range(D_FF // FF_CHUNK):
        cs = slice(c * FF_CHUNK, (c + 1) * FF_CHUNK)
        gate = _dot(x2b, wg_ref[:, cs])
        up = _dot(x2b, wu_ref[:, cs])
        hdn = (gate * (1.0 / (1.0 + jnp.exp(-gate))) * up).astype(bf16)
        part = _dot(hdn, wd_ref[cs, :])
        down = part if down is None else down + part
    o_ref[...] = _layer_norm(DEEPNORM_ALPHA * x2 + down, g3_ref[...], b3_ref[...])


def _post(xn, att, c, kc, vc, wo, g1, b1, wq, wom, g2, b2, wg, wu, wd, g3, b3, tiles_per_batch):
    t, d = xn.shape
    nt = t // POST_TM
    row = lambda i: (i, 0)
    const = lambda i: (0, 0)

    def resident(shape):
        return pl.BlockSpec(shape, const, pipeline_mode=pl.Buffered(1))

    return pl.pallas_call(
        _post_kernel,
        grid=(nt,),
        in_specs=[
            pl.BlockSpec((POST_TM, d), row),
            pl.BlockSpec((POST_TM, ATT_WIDTH), row),
            pl.BlockSpec((POST_TM, CONV_WIDTH), row),
            pl.BlockSpec((1, MEM_LEN, d), lambda i: (i // tiles_per_batch, 0, 0)),
            pl.BlockSpec((1, MEM_LEN, d), lambda i: (i // tiles_per_batch, 0, 0)),
            resident((d, d)), resident((1, d)), resident((1, d)),
            resident((d, d)), resident((d, d)), resident((1, d)), resident((1, d)),
            resident((d, D_FF)), resident((d, D_FF)), resident((D_FF, d)),
            resident((1, d)), resident((1, d)),
        ],
        out_specs=pl.BlockSpec((POST_TM, d), row),
        out_shape=jax.ShapeDtypeStruct((t, d), jnp.float32),
        scratch_shapes=[pltpu.VMEM((POST_TM, d), jnp.bfloat16)],
        compiler_params=pltpu.CompilerParams(
            dimension_semantics=("arbitrary",), vmem_limit_bytes=VMEM_LIMIT),
        name="post",
    )(xn, att, c, kc, vc, wo, g1, b1, wq, wom, g2, b2, wg, wu, wd, g3, b3)


def kernel(x, mem, in_norm_g, in_norm_b, w_in, lambda_q1, lambda_k1, lambda_q2, lambda_k2,
           subln_g, conv_w, conv_b, conv_norm_g, conv_norm_b, w_pw, b_pw, w_o, ln1_g, ln1_b,
           w_q_mem, w_kv_mem, w_o_mem, ln2_g, ln2_b, w_gate, w_up, w_down, ln3_g, ln3_b):
    bsz, s, d = x.shape
    bf16 = jnp.bfloat16
    row = lambda v: v.reshape(1, -1)
    slopes2 = jnp.asarray([2.0 ** (-8.0 * (i + 1) / ATT_HEADS) * LOG2E for i in range(ATT_HEADS)],
                          jnp.float32)
    g_in, b_in = row(in_norm_g), row(in_norm_b)
    for l in range(DEPTH):
        lam_init = 0.8 - 0.6 * math.exp(-0.3 * l)
        w = w_in[l].astype(bf16)
        wqT = w[:, :QK_TOTAL].T
        wk = w[:, QK_TOTAL:2 * QK_TOTAL]
        wvT = w[:, 2 * QK_TOTAL:2 * QK_TOTAL + ATT_WIDTH].T
        wc = w[:, 2 * QK_TOTAL + ATT_WIDTH:]
        xn, qT, k, vT, glu = _in_proj(x, g_in, b_in, wqT, wk, wvT, wc)
        att = _diff_attention(slopes2, qT, k, vT, row(lambda_q1[l]), row(lambda_k1[l]),
                              row(lambda_q2[l]), row(lambda_k2[l]),
                              subln_g[l].reshape(-1, 1), lam_init)
        c = _conv_module(glu, conv_w[l], row(conv_b[l]), row(conv_norm_g[l]),
                         row(conv_norm_b[l]), w_pw[l].astype(bf16), row(b_pw[l]))
        kc, vc = _mem_kv(mem, w_kv_mem[l].astype(bf16))
        out = _post(xn.reshape(bsz * s, d), att.reshape(bsz * s, ATT_WIDTH),
                    c.reshape(bsz * s, CONV_WIDTH), kc, vc,
                    w_o[l].astype(bf16), row(ln1_g[l]), row(ln1_b[l]),
                    w_q_mem[l].astype(bf16), w_o_mem[l].astype(bf16), row(ln2_g[l]), row(ln2_b[l]),
                    w_gate[l].astype(bf16), w_up[l].astype(bf16), w_down[l].astype(bf16),
                    row(ln3_g[l]), row(ln3_b[l]), s // POST_TM)
        x = out.reshape(bsz, s, d)
        g_in = b_in = None
    return x
```

```python
import functools
import math

import jax
import jax.numpy as jnp
from jax import lax
from jax.experimental import pallas as pl
from jax.experimental.pallas import tpu as pltpu

D_MODEL = 1024
DEPTH = 1
MEM_LEN = 256
ATT_WIDTH = 512
CONV_WIDTH = 512
ATT_HEADS = 4
ATT_HEAD_DIM = 128
QK_DIM = 64
QK_TOTAL = 512
CONV_GROUPS = 4
CONV_GROUP_DIM = 128
CONV_KERNEL = 31
MEM_HEADS = 4
MEM_HEAD_DIM = 256
D_FF = 2816
LN_EPS = 1e-5
DEEPNORM_ALPHA = (2 * DEPTH) ** 0.25

LOG2E = 1.4426950408889634
NEG_BIG = -1e30

ATT_BQ = 512
ATT_BK = 512
ATT_LT = 256
PROJ_TM = 512
CONV_TS = 256
CONV_HALO = 32
CONV_ROWS = 64
POST_TM = 512
FF_CHUNK = 1408

VMEM_LIMIT = 60 * 1024 * 1024

_NT = (((1,), (1,)), ((), ()))


def _layer_norm(x, g, b):
    mu = jnp.mean(x, axis=-1, keepdims=True)
    xc = x - mu
    var = jnp.mean(xc * xc, axis=-1, keepdims=True)
    return xc * lax.rsqrt(var + LN_EPS) * g + b


def _dot(a, b):
    return jnp.dot(a, b, preferred_element_type=jnp.float32)


def _dot_nt(a, b):
    return lax.dot_general(a, b, _NT, preferred_element_type=jnp.float32)


def _inproj_kernel(x_ref, g_ref, b_ref, wqT_ref, wk_ref, wvT_ref, wc_ref,
                   xn_ref, qT_ref, k_ref, vT_ref, glu_ref):
    xn = _layer_norm(x_ref[0], g_ref[...], b_ref[...])
    xn_ref[0] = xn
    xb = xn.astype(jnp.bfloat16)
    k_ref[0] = _dot(xb, wk_ref[...]).astype(jnp.bfloat16)
    qscale = (QK_DIM ** -0.5) * LOG2E
    qT = _dot_nt(wqT_ref[...], xb) * qscale
    vT = _dot_nt(wvT_ref[...], xb)
    for j in range(PROJ_TM // ATT_BQ):
        qT_ref[0, j] = qT[:, j * ATT_BQ:(j + 1) * ATT_BQ].astype(jnp.bfloat16)
    for j in range(PROJ_TM // ATT_BK):
        vT_ref[0, j] = vT[:, j * ATT_BK:(j + 1) * ATT_BK].astype(jnp.bfloat16)
    c_val = _dot(xb, wc_ref[:, :CONV_WIDTH])
    c_gate = _dot(xb, wc_ref[:, CONV_WIDTH:])
    glu_ref[0] = c_val * (1.0 / (1.0 + jnp.exp(-c_gate)))


def _in_proj(x, g, b, wqT, wk, wvT, wc):
    bsz, s, d = x.shape
    nt = s // PROJ_TM
    nbq = PROJ_TM // ATT_BQ
    nbk = PROJ_TM // ATT_BK
    const = lambda bi, si: (0, 0)
    return pl.pallas_call(
        _inproj_kernel,
        grid=(bsz, nt),
        in_specs=[
            pl.BlockSpec((1, PROJ_TM, d), lambda bi, si: (bi, si, 0)),
            pl.BlockSpec((1, d), const),
            pl.BlockSpec((1, d), const),
            pl.BlockSpec((QK_TOTAL, d), const),
            pl.BlockSpec((d, QK_TOTAL), const),
            pl.BlockSpec((ATT_WIDTH, d), const),
            pl.BlockSpec((d, 2 * CONV_WIDTH), const),
        ],
        out_specs=[
            pl.BlockSpec((1, PROJ_TM, d), lambda bi, si: (bi, si, 0)),
            pl.BlockSpec((1, nbq, QK_TOTAL, ATT_BQ), lambda bi, si: (bi, si, 0, 0)),
            pl.BlockSpec((1, PROJ_TM, QK_TOTAL), lambda bi, si: (bi, si, 0)),
            pl.BlockSpec((1, nbk, ATT_WIDTH, ATT_BK), lambda bi, si: (bi, si, 0, 0)),
            pl.BlockSpec((1, PROJ_TM, CONV_WIDTH), lambda bi, si: (bi, si, 0)),
        ],
        out_shape=[
            jax.ShapeDtypeStruct((bsz, s, d), jnp.float32),
            jax.ShapeDtypeStruct((bsz, s // ATT_BQ, QK_TOTAL, ATT_BQ), jnp.bfloat16),
            jax.ShapeDtypeStruct((bsz, s, QK_TOTAL), jnp.bfloat16),
            jax.ShapeDtypeStruct((bsz, s // ATT_BK, ATT_WIDTH, ATT_BK), jnp.bfloat16),
            jax.ShapeDtypeStruct((bsz, s, CONV_WIDTH), jnp.float32),
        ],
        compiler_params=pltpu.CompilerParams(
            dimension_semantics=("arbitrary", "arbitrary"),
            vmem_limit_bytes=VMEM_LIMIT),
        name="in_proj",
    )(x, g, b, wqT, wk, wvT, wc)


def _attn_kernel(slope_ref, qT_ref, k_ref, vT_ref, lq1_ref, lk1_ref, lq2_ref, lk2_ref,
                 gain_ref, o_ref, qs_ref, base_ref, m_ref, l_ref, acc_ref,
                 sa_ref, sb_ref, mxa_ref, mxb_ref, *, lam_init):
    bq, bk = ATT_BQ, ATT_BK
    h = pl.program_id(1)
    qi = pl.program_id(2)
    slope2 = slope_ref[h]

    qT = qT_ref[0, 0]
    row = lax.broadcasted_iota(jnp.int32, (ATT_HEAD_DIM, bq), 0)
    zero = jnp.zeros_like(qT)
    qs_ref[:, :bq] = jnp.where(row < QK_DIM, qT, zero)
    qs_ref[:, bq:] = jnp.where(row >= QK_DIM, qT, zero)

    kr = lax.broadcasted_iota(jnp.int32, (bk, bq), 0)
    qc = lax.broadcasted_iota(jnp.int32, (bk, bq), 1)
    rel = kr - qc
    base = slope2 * rel.astype(jnp.float32)
    base_ref[:, :bq] = base
    base_ref[:, bq:] = base

    m_ref[...] = jnp.full(m_ref.shape, NEG_BIG, jnp.float32)
    l_ref[...] = jnp.zeros(l_ref.shape, jnp.float32)
    acc_ref[...] = jnp.zeros(acc_ref.shape, jnp.float32)

    q0 = qi * bq
    n = q0 // bk

    def produce(j, s_ref, mx_ref, masked):
        start = pl.multiple_of(j * bk, bk)
        kb = k_ref[0, pl.ds(start, bk), :]
        s = _dot(kb, qs_ref[...]) + base_ref[...]
        if masked:
            valid = rel <= (q0 - j * bk)
            s = jnp.where(jnp.concatenate([valid, valid], axis=1), s, NEG_BIG)
        s_ref[...] = s
        mx_ref[...] = jnp.max(s, axis=0, keepdims=True)

    def consume(j, s_ref, mx_ref):
        off = slope2 * (j * bk - q0).astype(jnp.float32)
        m_old = m_ref[...]
        m_new = jnp.maximum(m_old, mx_ref[...] + off)
        p = jnp.exp2(s_ref[...] - (m_new - off))
        alpha = jnp.exp2(m_old - m_new)
        l_ref[...] = alpha * l_ref[...] + jnp.sum(p, axis=0, keepdims=True)
        acc_ref[...] = alpha * acc_ref[...] + _dot(vT_ref[0, j], p.astype(jnp.bfloat16))
        m_ref[...] = m_new

    bufs = ((sa_ref, mxa_ref), (sb_ref, mxb_ref))

    @pl.when(n == 0)
    def _():
        produce(0, *bufs[0], True)

    @pl.when(n > 0)
    def _():
        produce(0, *bufs[0], False)

    npair = jnp.maximum(n - 1, 0) // 2

    def pair(i, carry):
        j = 2 * i
        produce(j + 1, *bufs[1], False)
        consume(j, *bufs[0])
        produce(j + 2, *bufs[0], False)
        consume(j + 1, *bufs[1])
        return carry

    lax.fori_loop(0, npair, pair, 0)
    j0 = 2 * npair
    rest = n - j0

    @pl.when(rest == 0)
    def _():
        consume(0, *bufs[0])

    @pl.when(rest == 1)
    def _():
        produce(n, *bufs[1], True)
        consume(j0, *bufs[0])
        consume(n, *bufs[1])

    @pl.when(rest == 2)
    def _():
        produce(j0 + 1, *bufs[1], False)
        consume(j0, *bufs[0])
        produce(n, *bufs[0], True)
        consume(j0 + 1, *bufs[1])
        consume(n, *bufs[0])

    lam = (jnp.exp(jnp.sum(lq1_ref[...] * lk1_ref[...], axis=-1, keepdims=True))
           - jnp.exp(jnp.sum(lq2_ref[...] * lk2_ref[...], axis=-1, keepdims=True))
           + lam_init)
    acc = acc_ref[...]
    l = l_ref[...]
    oT = acc[:, :bq] / l[:, :bq] - lam * (acc[:, bq:] / l[:, bq:])
    ms = jnp.mean(oT * oT, axis=0, keepdims=True)
    oT = oT * lax.rsqrt(ms + LN_EPS) * gain_ref[...] * (1.0 - lam_init)
    o_ref[0] = oT.T.astype(o_ref.dtype)


def _diff_attention(slopes2, qT, k, vT, lq1, lk1, lq2, lk2, gain_col, lam_init):
    bsz, nq, _, bq = qT.shape
    _, nk, _, bk = vT.shape
    s = nq * bq
    vec = lambda bi, h, qi: (0, 0)
    return pl.pallas_call(
        functools.partial(_attn_kernel, lam_init=lam_init),
        grid=(bsz, ATT_HEADS, nq),
        in_specs=[
            pl.BlockSpec(memory_space=pltpu.SMEM),
            pl.BlockSpec((1, 1, ATT_HEAD_DIM, bq), lambda bi, h, qi: (bi, qi, h, 0)),
            pl.BlockSpec((1, s, ATT_HEAD_DIM), lambda bi, h, qi: (bi, 0, h)),
            pl.BlockSpec((1, nk, ATT_HEAD_DIM, bk), lambda bi, h, qi: (bi, 0, h, 0)),
            pl.BlockSpec((1, QK_DIM), vec),
            pl.BlockSpec((1, QK_DIM), vec),
            pl.BlockSpec((1, QK_DIM), vec),
            pl.BlockSpec((1, QK_DIM), vec),
            pl.BlockSpec((ATT_HEAD_DIM, 1), vec),
        ],
        out_specs=pl.BlockSpec((1, bq, ATT_HEAD_DIM), lambda bi, h, qi: (bi, qi, h)),
        out_shape=jax.ShapeDtypeStruct((bsz, s, ATT_WIDTH), jnp.bfloat16),
        scratch_shapes=[
            pltpu.VMEM((ATT_HEAD_DIM, 2 * bq), jnp.bfloat16),
            pltpu.VMEM((bk, 2 * bq), jnp.float32),
            pltpu.VMEM((1, 2 * bq), jnp.float32),
            pltpu.VMEM((1, 2 * bq), jnp.float32),
            pltpu.VMEM((ATT_HEAD_DIM, 2 * bq), jnp.float32),
            pltpu.VMEM((bk, 2 * bq), jnp.float32),
            pltpu.VMEM((bk, 2 * bq), jnp.float32),
            pltpu.VMEM((1, 2 * bq), jnp.float32),
            pltpu.VMEM((1, 2 * bq), jnp.float32),
        ],
        compiler_params=pltpu.CompilerParams(
            dimension_semantics=("arbitrary", "arbitrary", "arbitrary"),
            vmem_limit_bytes=VMEM_LIMIT),
        name="diff_attn",
    )(slopes2, qT, k, vT, lq1, lk1, lq2, lk2, gain_col)


def _conv_kernel(prev_ref, cur_ref, w_ref, cb_ref, ng_ref, nb_ref, wpw_ref, bpw_ref,
                 o_ref, xx_ref, act_ref):
    si = pl.program_id(1)
    prev = prev_ref[0]
    xx_ref[:CONV_HALO] = jnp.where(si > 0, prev, jnp.zeros_like(prev))
    xx_ref[CONV_HALO:] = cur_ref[0]
    first = CONV_HALO - (CONV_KERNEL - 1)
    for rc in range(CONV_TS // CONV_ROWS):
        r0 = rc * CONV_ROWS
        acc = jnp.broadcast_to(cb_ref[...], (CONV_ROWS, CONV_WIDTH))
        for j in range(CONV_KERNEL):
            acc = acc + w_ref[j:j + 1, :] * xx_ref[r0 + first + j:r0 + first + j + CONV_ROWS, :]
        for g in range(CONV_GROUPS):
            gs = slice(g * CONV_GROUP_DIM, (g + 1) * CONV_GROUP_DIM)
            y = _layer_norm(acc[:, gs], ng_ref[:, gs], nb_ref[:, gs])
            y = y * (1.0 / (1.0 + jnp.exp(-y)))
            act_ref[r0:r0 + CONV_ROWS, gs] = y.astype(jnp.bfloat16)
    o_ref[0] = (_dot(act_ref[...], wpw_ref[...]) + bpw_ref[...]).astype(o_ref.dtype)


def _conv_module(glu, w, cb, ng, nb, wpw, bpw):
    bsz, s, c = glu.shape
    nt = s // CONV_TS
    ratio = CONV_TS // CONV_HALO
    const = lambda bi, si: (0, 0)
    return pl.pallas_call(
        _conv_kernel,
        grid=(bsz, nt),
        in_specs=[
            pl.BlockSpec((1, CONV_HALO, c), lambda bi, si: (bi, jnp.maximum(si * ratio - 1, 0), 0)),
            pl.BlockSpec((1, CONV_TS, c), lambda bi, si: (bi, si, 0)),
            pl.BlockSpec((CONV_KERNEL, c), const),
            pl.BlockSpec((1, c), const),
            pl.BlockSpec((1, c), const),
            pl.BlockSpec((1, c), const),
            pl.BlockSpec((c, c), const),
            pl.BlockSpec((1, c), const),
        ],
        out_specs=pl.BlockSpec((1, CONV_TS, c), lambda bi, si: (bi, si, 0)),
        out_shape=jax.ShapeDtypeStruct((bsz, s, c), jnp.bfloat16),
        scratch_shapes=[
            pltpu.VMEM((CONV_HALO + CONV_TS, c), jnp.float32),
            pltpu.VMEM((CONV_TS, c), jnp.bfloat16),
        ],
        compiler_params=pltpu.CompilerParams(
            dimension_semantics=("arbitrary", "arbitrary"),
            vmem_limit_bytes=VMEM_LIMIT),
        name="conv_mod",
    )(glu, glu, w, cb, ng, nb, wpw, bpw)


def _memkv_kernel(mem_ref, w_ref, kc_ref, vc_ref):
    mb = mem_ref[0].astype(jnp.bfloat16)
    kc_ref[0] = _dot(mb, w_ref[:, :D_MODEL]).astype(jnp.bfloat16)
    vc_ref[0] = _dot(mb, w_ref[:, D_MODEL:]).astype(jnp.bfloat16)


def _mem_kv(mem, w_kv):
    bsz, m, d = mem.shape
    return pl.pallas_call(
        _memkv_kernel,
        grid=(bsz,),
        in_specs=[
            pl.BlockSpec((1, m, d), lambda bi: (bi, 0, 0)),
            pl.BlockSpec((d, 2 * d), lambda bi: (0, 0)),
        ],
        out_specs=[
            pl.BlockSpec((1, m, d), lambda bi: (bi, 0, 0)),
            pl.BlockSpec((1, m, d), lambda bi: (bi, 0, 0)),
        ],
        out_shape=[
            jax.ShapeDtypeStruct((bsz, m, d), jnp.bfloat16),
            jax.ShapeDtypeStruct((bsz, m, d), jnp.bfloat16),
        ],
        compiler_params=pltpu.CompilerParams(
            dimension_semantics=("arbitrary",), vmem_limit_bytes=VMEM_LIMIT),
        name="mem_kv",
    )(mem, w_kv)


def _post_kernel(xn_ref, att_ref, c_ref, kc_ref, vc_ref,
                 wo_ref, g1_ref, b1_ref, wq_ref, wom_ref, g2_ref, b2_ref,
                 wg_ref, wu_ref, wd_ref, g3_ref, b3_ref, o_ref, ca_ref):
    bf16 = jnp.bfloat16
    mix = _dot(att_ref[...], wo_ref[:ATT_WIDTH, :]) + _dot(c_ref[...], wo_ref[ATT_WIDTH:, :])
    x1 = _layer_norm(DEEPNORM_ALPHA * xn_ref[...] + mix, g1_ref[...], b1_ref[...])

    qscale = (MEM_HEAD_DIM ** -0.5) * LOG2E
    qm = (_dot(x1.astype(bf16), wq_ref[...]) * qscale).astype(bf16)
    for h in range(MEM_HEADS):
        hs = slice(h * MEM_HEAD_DIM, (h + 1) * MEM_HEAD_DIM)
        logits = _dot_nt(qm[:, hs], kc_ref[0, :, hs])
        p = jnp.exp2(logits - jnp.max(logits, axis=-1, keepdims=True))
        denom = jnp.sum(p, axis=-1, keepdims=True)
        ca = _dot(p.astype(bf16), vc_ref[0, :, hs]) / denom
        ca_ref[:, hs] = ca.astype(bf16)
    x2 = _layer_norm(DEEPNORM_ALPHA * x1 + _dot(ca_ref[...], wom_ref[...]),
                     g2_ref[...], b2_ref[...])

    x2b = x2.astype(bf16)
    down = None
    for c in range(D_FF // FF_CHUNK):
        cs = slice(c * FF_CHUNK, (c + 1) * FF_CHUNK)
        gate = _dot(x2b, wg_ref[:, cs])
        up = _dot(x2b, wu_ref[:, cs])
        hdn = (gate * (1.0 / (1.0 + jnp.exp(-gate))) * up).astype(bf16)
        part = _dot(hdn, wd_ref[cs, :])
        down = part if down is None else down + part
    o_ref[...] = _layer_norm(DEEPNORM_ALPHA * x2 + down, g3_ref[...], b3_ref[...])


def _post(xn, att, c, kc, vc, wo, g1, b1, wq, wom, g2, b2, wg, wu, wd, g3, b3, tiles_per_batch):
    t, d = xn.shape
    nt = t // POST_TM
    row = lambda i: (i, 0)
    const = lambda i: (0, 0)

    def resident(shape):
        return pl.BlockSpec(shape, const, pipeline_mode=pl.Buffered(1))

    return pl.pallas_call(
        _post_kernel,
        grid=(nt,),
        in_specs=[
            pl.BlockSpec((POST_TM, d), row),
            pl.BlockSpec((POST_TM, ATT_WIDTH), row),
            pl.BlockSpec((POST_TM, CONV_WIDTH), row),
            pl.BlockSpec((1, MEM_LEN, d), lambda i: (i // tiles_per_batch, 0, 0)),
            pl.BlockSpec((1, MEM_LEN, d), lambda i: (i // tiles_per_batch, 0, 0)),
            resident((d, d)), resident((1, d)), resident((1, d)),
            resident((d, d)), resident((d, d)), resident((1, d)), resident((1, d)),
            resident((d, D_FF)), resident((d, D_FF)), resident((D_FF, d)),
            resident((1, d)), resident((1, d)),
        ],
        out_specs=pl.BlockSpec((POST_TM, d), row),
        out_shape=jax.ShapeDtypeStruct((t, d), jnp.float32),
        scratch_shapes=[pltpu.VMEM((POST_TM, d), jnp.bfloat16)],
        compiler_params=pltpu.CompilerParams(
            dimension_semantics=("arbitrary",), vmem_limit_bytes=VMEM_LIMIT),
        name="post",
    )(xn, att, c, kc, vc, wo, g1, b1, wq, wom, g2, b2, wg, wu, wd, g3, b3)


def kernel(x, mem, in_norm_g, in_norm_b, w_in, lambda_q1, lambda_k1, lambda_q2, lambda_k2,
           subln_g, conv_w, conv_b, conv_norm_g, conv_norm_b, w_pw, b_pw, w_o, ln1_g, ln1_b,
           w_q_mem, w_kv_mem, w_o_mem, ln2_g, ln2_b, w_gate, w_up, w_down, ln3_g, ln3_b):
    bsz, s, d = x.shape
    bf16 = jnp.bfloat16
    row = lambda v: v.reshape(1, -1)
    slopes2 = jnp.asarray([2.0 ** (-8.0 * (i + 1) / ATT_HEADS) * LOG2E for i in range(ATT_HEADS)],
                          jnp.float32)
    g_in, b_in = row(in_norm_g), row(in_norm_b)
    for l in range(DEPTH):
        lam_init = 0.8 - 0.6 * math.exp(-0.3 * l)
        w = w_in[l].astype(bf16)
        wqT = w[:, :QK_TOTAL].T
        wk = w[:, QK_TOTAL:2 * QK_TOTAL]
        wvT = w[:, 2 * QK_TOTAL:2 * QK_TOTAL + ATT_WIDTH].T
        wc = w[:, 2 * QK_TOTAL + ATT_WIDTH:]
        xn, qT, k, vT, glu = _in_proj(x, g_in, b_in, wqT, wk, wvT, wc)
        att = _diff_attention(slopes2, qT, k, vT, row(lambda_q1[l]), row(lambda_k1[l]),
                              row(lambda_q2[l]), row(lambda_k2[l]),
                              subln_g[l].reshape(-1, 1), lam_init)
        c = _conv_module(glu, conv_w[l], row(conv_b[l]), row(conv_norm_g[l]),
                         row(conv_norm_b[l]), w_pw[l].astype(bf16), row(b_pw[l]))
        kc, vc = _mem_kv(mem, w_kv_mem[l].astype(bf16))
        out = _post(xn.reshape(bsz * s, d), att.reshape(bsz * s, ATT_WIDTH),
                    c.reshape(bsz * s, CONV_WIDTH), kc, vc,
                    w_o[l].astype(bf16), row(ln1_g[l]), row(ln1_b[l]),
                    w_q_mem[l].astype(bf16), w_o_mem[l].astype(bf16), row(ln2_g[l]), row(ln2_b[l]),
                    w_gate[l].astype(bf16), w_up[l].astype(bf16), w_down[l].astype(bf16),
                    row(ln3_g[l]), row(ln3_b[l]), s // POST_TM)
        x = out.reshape(bsz, s, d)
        g_in = b_in = None
    return x
```

```python
import functools
import math

import jax
import jax.numpy as jnp
from jax import lax
from jax.experimental import pallas as pl
from jax.experimental.pallas import tpu as pltpu

D_MODEL = 1024
DEPTH = 1
MEM_LEN = 256
ATT_WIDTH = 512
CONV_WIDTH = 512
ATT_HEADS = 4
ATT_HEAD_DIM = 128
QK_DIM = 64
QK_TOTAL = 512
CONV_GROUPS = 4
CONV_GROUP_DIM = 128
CONV_KERNEL = 31
MEM_HEADS = 4
MEM_HEAD_DIM = 256
D_FF = 2816
LN_EPS = 1e-5
DEEPNORM_ALPHA = (2 * DEPTH) ** 0.25

LOG2E = 1.4426950408889634
NEG_BIG = -1e30

ATT_BQ = 512
ATT_BK = 512
ATT_SUM_ROWS = 16
PROJ_TM = 512
CONV_TS = 256
CONV_HALO = 32
CONV_ROWS = 64
POST_TM = 512
FF_CHUNK = 1408

VMEM_LIMIT = 60 * 1024 * 1024

_NT = (((1,), (1,)), ((), ()))


def _layer_norm(x, g, b):
    mu = jnp.mean(x, axis=-1, keepdims=True)
    xc = x - mu
    var = jnp.mean(xc * xc, axis=-1, keepdims=True)
    return xc * lax.rsqrt(var + LN_EPS) * g + b


def _alibi_slope(h):
    return 2.0 ** (-8.0 * (h + 1) / ATT_HEADS)


def _dot(a, b):
    return jnp.dot(a, b, preferred_element_type=jnp.float32)


def _dot_nt(a, b):
    return lax.dot_general(a, b, _NT, preferred_element_type=jnp.float32)


def _inproj_kernel(x_ref, g_ref, b_ref, wqT_ref, wk_ref, wvT_ref, wc_ref,
                   xn_ref, qT_ref, k_ref, vT_ref, glu_ref):
    xn = _layer_norm(x_ref[0], g_ref[...], b_ref[...])
    xn_ref[0] = xn
    xb = xn.astype(jnp.bfloat16)
    kb = _dot(xb, wk_ref[...]).astype(jnp.bfloat16)
    pos = lax.broadcasted_iota(jnp.int32, (PROJ_TM, ATT_HEAD_DIM), 0) % ATT_BK
    lane = lax.broadcasted_iota(jnp.int32, (PROJ_TM, ATT_HEAD_DIM), 1)
    for h in range(ATT_HEADS):
        u = (_alibi_slope(h) * LOG2E) * pos.astype(jnp.float32)
        hi = u.astype(jnp.bfloat16).astype(jnp.float32)
        mid = (u - hi).astype(jnp.bfloat16).astype(jnp.float32)
        lo = u - hi - mid
        bias = jnp.where(lane == 0, hi, jnp.where(lane == 1, mid, jnp.where(lane == 2, lo, 0.0)))
        k_ref[0, :, 2 * h * ATT_HEAD_DIM:(2 * h + 1) * ATT_HEAD_DIM] = (
            kb[:, h * ATT_HEAD_DIM:(h + 1) * ATT_HEAD_DIM])
        k_ref[0, :, (2 * h + 1) * ATT_HEAD_DIM:(2 * h + 2) * ATT_HEAD_DIM] = bias.astype(jnp.bfloat16)
    qscale = (QK_DIM ** -0.5) * LOG2E
    qT = _dot_nt(wqT_ref[...], xb) * qscale
    vT = _dot_nt(wvT_ref[...], xb)
    for j in range(PROJ_TM // ATT_BQ):
        qT_ref[0, j] = qT[:, j * ATT_BQ:(j + 1) * ATT_BQ].astype(jnp.bfloat16)
    for j in range(PROJ_TM // ATT_BK):
        vT_ref[0, j] = vT[:, j * ATT_BK:(j + 1) * ATT_BK].astype(jnp.bfloat16)
    c_val = _dot(xb, wc_ref[:, :CONV_WIDTH])
    c_gate = _dot(xb, wc_ref[:, CONV_WIDTH:])
    glu_ref[0] = c_val * (1.0 / (1.0 + jnp.exp(-c_gate)))


def _in_proj(x, g, b, wqT, wk, wvT, wc):
    bsz, s, d = x.shape
    nt = s // PROJ_TM
    nbq = PROJ_TM // ATT_BQ
    nbk = PROJ_TM // ATT_BK
    const = lambda bi, si: (0, 0)
    return pl.pallas_call(
        _inproj_kernel,
        grid=(bsz, nt),
        in_specs=[
            pl.BlockSpec((1, PROJ_TM, d), lambda bi, si: (bi, si, 0)),
            pl.BlockSpec((1, d), const),
            pl.BlockSpec((1, d), const),
            pl.BlockSpec((QK_TOTAL, d), const),
            pl.BlockSpec((d, QK_TOTAL), const),
            pl.BlockSpec((ATT_WIDTH, d), const),
            pl.BlockSpec((d, 2 * CONV_WIDTH), const),
        ],
        out_specs=[
            pl.BlockSpec((1, PROJ_TM, d), lambda bi, si: (bi, si, 0)),
            pl.BlockSpec((1, nbq, QK_TOTAL, ATT_BQ), lambda bi, si: (bi, si, 0, 0)),
            pl.BlockSpec((1, PROJ_TM, 2 * QK_TOTAL), lambda bi, si: (bi, si, 0)),
            pl.BlockSpec((1, nbk, ATT_WIDTH, ATT_BK), lambda bi, si: (bi, si, 0, 0)),
            pl.BlockSpec((1, PROJ_TM, CONV_WIDTH), lambda bi, si: (bi, si, 0)),
        ],
        out_shape=[
            jax.ShapeDtypeStruct((bsz, s, d), jnp.float32),
            jax.ShapeDtypeStruct((bsz, s // ATT_BQ, QK_TOTAL, ATT_BQ), jnp.bfloat16),
            jax.ShapeDtypeStruct((bsz, s, 2 * QK_TOTAL), jnp.bfloat16),
            jax.ShapeDtypeStruct((bsz, s // ATT_BK, ATT_WIDTH, ATT_BK), jnp.bfloat16),
            jax.ShapeDtypeStruct((bsz, s, CONV_WIDTH), jnp.float32),
        ],
        compiler_params=pltpu.CompilerParams(
            dimension_semantics=("arbitrary", "arbitrary"),
            vmem_limit_bytes=VMEM_LIMIT),
        name="in_proj",
    )(x, g, b, wqT, wk, wvT, wc)


def _attn_kernel(slope_ref, qT_ref, k_ref, vT_ref, lq1_ref, lk1_ref, lq2_ref, lk2_ref,
                 gain_ref, o_ref, qs_ref, m_ref, acc_ref,
                 sa_ref, sb_ref, mxa_ref, mxb_ref, *, lam_init):
    bq, bk = ATT_BQ, ATT_BK
    h = pl.program_id(1)
    qi = pl.program_id(2)
    slope2 = slope_ref[h]

    qT = qT_ref[0, 0]
    row = lax.broadcasted_iota(jnp.int32, (ATT_HEAD_DIM, bq), 0)
    zero = jnp.zeros_like(qT)
    qs_ref[:ATT_HEAD_DIM, :bq] = jnp.where(row < QK_DIM, qT, zero)
    qs_ref[:ATT_HEAD_DIM, bq:] = jnp.where(row >= QK_DIM, qT, zero)
    row2 = lax.broadcasted_iota(jnp.int32, (ATT_HEAD_DIM, 2 * bq), 0)
    qs_ref[ATT_HEAD_DIM:, :] = jnp.where(row2 < 3, 1.0, 0.0).astype(jnp.bfloat16)

    m_ref[...] = jnp.full(m_ref.shape, NEG_BIG, jnp.float32)
    acc_ref[...] = jnp.zeros(acc_ref.shape, jnp.float32)

    q0 = qi * bq
    n = q0 // bk
    ones_rows = jnp.ones((ATT_SUM_ROWS, bk), jnp.bfloat16)

    def produce(j, s_ref, mx_ref, masked):
        start = pl.multiple_of(j * bk, bk)
        kb = k_ref[0, pl.ds(start, bk), :]
        s = _dot(kb, qs_ref[...])
        if masked:
            kr = lax.broadcasted_iota(jnp.int32, (bk, bq), 0)
            qc = lax.broadcasted_iota(jnp.int32, (bk, bq), 1)
            valid = (kr - qc) <= (q0 - j * bk)
            s = jnp.where(jnp.concatenate([valid, valid], axis=1), s, NEG_BIG)
        s_ref[...] = s
        mx_ref[...] = jnp.max(s, axis=0, keepdims=True)

    def consume(j, s_ref, mx_ref):
        off = slope2 * (j * bk - q0).astype(jnp.float32)
        m_old = m_ref[...]
        m_new = jnp.maximum(m_old, mx_ref[...] + off)
        p = jnp.exp2(s_ref[...] - (m_new - off)).astype(jnp.bfloat16)
        alpha = jnp.exp2(m_old - m_new)
        lhs = jnp.concatenate([vT_ref[0, j], ones_rows], axis=0)
        acc_ref[...] = alpha * acc_ref[...] + _dot(lhs, p)
        m_ref[...] = m_new

    bufs = ((sa_ref, mxa_ref), (sb_ref, mxb_ref))

    @pl.when(n == 0)
    def _():
        produce(0, *bufs[0], True)

    @pl.when(n > 0)
    def _():
        produce(0, *bufs[0], False)

    npair = jnp.maximum(n - 1, 0) // 2

    def pair(i, carry):
        j = 2 * i
        produce(j + 1, *bufs[1], False)
        consume(j, *bufs[0])
        produce(j + 2, *bufs[0], False)
        consume(j + 1, *bufs[1])
        return carry

    lax.fori_loop(0, npair, pair, 0)
    j0 = 2 * npair
    rest = n - j0

    @pl.when(rest == 0)
    def _():
        consume(0, *bufs[0])

    @pl.when(rest == 1)
    def _():
        produce(n, *bufs[1], True)
        consume(j0, *bufs[0])
        consume(n, *bufs[1])

    @pl.when(rest == 2)
    def _():
        produce(j0 + 1, *bufs[1], False)
        consume(j0, *bufs[0])
        produce(n, *bufs[0], True)
        consume(j0 + 1, *bufs[1])
        consume(n, *bufs[0])

    lam = (jnp.exp(jnp.sum(lq1_ref[...] * lk1_ref[...], axis=-1, keepdims=True))
           - jnp.exp(jnp.sum(lq2_ref[...] * lk2_ref[...], axis=-1, keepdims=True))
           + lam_init)
    acc = acc_ref[:ATT_HEAD_DIM, :]
    l = acc_ref[ATT_HEAD_DIM:ATT_HEAD_DIM + 1, :]
    oT = acc[:, :bq] / l[:, :bq] - lam * (acc[:, bq:] / l[:, bq:])
    ms = jnp.mean(oT * oT, axis=0, keepdims=True)
    oT = oT * lax.rsqrt(ms + LN_EPS) * gain_ref[...] * (1.0 - lam_init)
    o_ref[0] = oT.T.astype(o_ref.dtype)


def _diff_attention(slopes2, qT, k, vT, lq1, lk1, lq2, lk2, gain_col, lam_init):
    bsz, nq, _, bq = qT.shape
    _, nk, _, bk = vT.shape
    s = nq * bq
    vec = lambda bi, h, qi: (0, 0)
    return pl.pallas_call(
        functools.partial(_attn_kernel, lam_init=lam_init),
        grid=(bsz, ATT_HEADS, nq),
        in_specs=[
            pl.BlockSpec(memory_space=pltpu.SMEM),
            pl.BlockSpec((1, 1, ATT_HEAD_DIM, bq), lambda bi, h, qi: (bi, qi, h, 0)),
            pl.BlockSpec((1, s, 2 * ATT_HEAD_DIM), lambda bi, h, qi: (bi, 0, h)),
            pl.BlockSpec((1, nk, ATT_HEAD_DIM, bk), lambda bi, h, qi: (bi, 0, h, 0)),
            pl.BlockSpec((1, QK_DIM), vec),
            pl.BlockSpec((1, QK_DIM), vec),
            pl.BlockSpec((1, QK_DIM), vec),
            pl.BlockSpec((1, QK_DIM), vec),
            pl.BlockSpec((ATT_HEAD_DIM, 1), vec),
        ],
        out_specs=pl.BlockSpec((1, bq, ATT_HEAD_DIM), lambda bi, h, qi: (bi, qi, h)),
        out_shape=jax.ShapeDtypeStruct((bsz, s, ATT_WIDTH), jnp.bfloat16),
        scratch_shapes=[
            pltpu.VMEM((2 * ATT_HEAD_DIM, 2 * bq), jnp.bfloat16),
            pltpu.VMEM((1, 2 * bq), jnp.float32),
            pltpu.VMEM((ATT_HEAD_DIM + ATT_SUM_ROWS, 2 * bq), jnp.float32),
            pltpu.VMEM((bk, 2 * bq), jnp.float32),
            pltpu.VMEM((bk, 2 * bq), jnp.float32),
            pltpu.VMEM((1, 2 * bq), jnp.float32),
            pltpu.VMEM((1, 2 * bq), jnp.float32),
        ],
        compiler_params=pltpu.CompilerParams(
            dimension_semantics=("arbitrary", "arbitrary", "arbitrary"),
            vmem_limit_bytes=VMEM_LIMIT),
        name="diff_attn",
    )(slopes2, qT, k, vT, lq1, lk1, lq2, lk2, gain_col)


def _conv_kernel(prev_ref, cur_ref, w_ref, cb_ref, ng_ref, nb_ref, wpw_ref, bpw_ref,
                 o_ref, xx_ref, act_ref):
    si = pl.program_id(1)
    prev = prev_ref[0]
    prev = jnp.where(si > 0, prev, jnp.zeros_like(prev))
    first = CONV_HALO - (CONV_KERNEL - 1)
    nchunk = CONV_TS // CONV_ROWS
    for g in range(CONV_GROUPS):
        gs = slice(g * CONV_GROUP_DIM, (g + 1) * CONV_GROUP_DIM)
        xx_ref[g, :CONV_HALO] = prev[:, gs]
        xx_ref[g, CONV_HALO:] = cur_ref[0, :, gs]
    for g in range(CONV_GROUPS):
        gs = slice(g * CONV_GROUP_DIM, (g + 1) * CONV_GROUP_DIM)
        accs = [jnp.broadcast_to(cb_ref[:, gs], (CONV_ROWS, CONV_GROUP_DIM))] * nchunk
        for j in range(CONV_KERNEL):
            wj = jnp.broadcast_to(w_ref[j:j + 1, gs], (CONV_ROWS, CONV_GROUP_DIM))
            for rc in range(nchunk):
                r = rc * CONV_ROWS + first + j
                accs[rc] = accs[rc] + wj * xx_ref[g, r:r + CONV_ROWS, :]
        for rc in range(nchunk):
            y = _layer_norm(accs[rc], ng_ref[:, gs], nb_ref[:, gs])
            y = y * (1.0 / (1.0 + jnp.exp(-y)))
            act_ref[rc * CONV_ROWS:(rc + 1) * CONV_ROWS, gs] = y.astype(jnp.bfloat16)
    o_ref[0] = (_dot(act_ref[...], wpw_ref[...]) + bpw_ref[...]).astype(o_ref.dtype)


def _conv_module(glu, w, cb, ng, nb, wpw, bpw):
    bsz, s, c = glu.shape
    nt = s // CONV_TS
    ratio = CONV_TS // CONV_HALO
    const = lambda bi, si: (0, 0)
    return pl.pallas_call(
        _conv_kernel,
        grid=(bsz, nt),
        in_specs=[
            pl.BlockSpec((1, CONV_HALO, c), lambda bi, si: (bi, jnp.maximum(si * ratio - 1, 0), 0)),
            pl.BlockSpec((1, CONV_TS, c), lambda bi, si: (bi, si, 0)),
            pl.BlockSpec((CONV_KERNEL, c), const),
            pl.BlockSpec((1, c), const),
            pl.BlockSpec((1, c), const),
            pl.BlockSpec((1, c), const),
            pl.BlockSpec((c, c), const),
            pl.BlockSpec((1, c), const),
        ],
        out_specs=pl.BlockSpec((1, CONV_TS, c), lambda bi, si: (bi, si, 0)),
        out_shape=jax.ShapeDtypeStruct((bsz, s, c), jnp.bfloat16),
        scratch_shapes=[
            pltpu.VMEM((CONV_GROUPS, CONV_HALO + CONV_TS, CONV_GROUP_DIM), jnp.float32),
            pltpu.VMEM((CONV_TS, c), jnp.bfloat16),
        ],
        compiler_params=pltpu.CompilerParams(
            dimension_semantics=("arbitrary", "arbitrary"),
            vmem_limit_bytes=VMEM_LIMIT),
        name="conv_mod",
    )(glu, glu, w, cb, ng, nb, wpw, bpw)


def _memkv_kernel(mem_ref, w_ref, kc_ref, vc_ref):
    mb = mem_ref[0].astype(jnp.bfloat16)
    kc_ref[0] = _dot(mb, w_ref[:, :D_MODEL]).astype(jnp.bfloat16)
    vc_ref[0] = _dot(mb, w_ref[:, D_MODEL:]).astype(jnp.bfloat16)


def _mem_kv(mem, w_kv):
    bsz, m, d = mem.shape
    return pl.pallas_call(
        _memkv_kernel,
        grid=(bsz,),
        in_specs=[
            pl.BlockSpec((1, m, d), lambda bi: (bi, 0, 0)),
            pl.BlockSpec((d, 2 * d), lambda bi: (0, 0)),
        ],
        out_specs=[
            pl.BlockSpec((1, m, d), lambda bi: (bi, 0, 0)),
            pl.BlockSpec((1, m, d), lambda bi: (bi, 0, 0)),
        ],
        out_shape=[
            jax.ShapeDtypeStruct((bsz, m, d), jnp.bfloat16),
            jax.ShapeDtypeStruct((bsz, m, d), jnp.bfloat16),
        ],
        compiler_params=pltpu.CompilerParams(
            dimension_semantics=("arbitrary",), vmem_limit_bytes=VMEM_LIMIT),
        name="mem_kv",
    )(mem, w_kv)


def _post_kernel(xn_ref, att_ref, c_ref, kc_ref, vc_ref,
                 wo_ref, g1_ref, b1_ref, wq_ref, wom_ref, g2_ref, b2_ref,
                 wg_ref, wu_ref, wd_ref, g3_ref, b3_ref, o_ref, ca_ref):
    bf16 = jnp.bfloat16
    mix = _dot(att_ref[...], wo_ref[:ATT_WIDTH, :]) + _dot(c_ref[...], wo_ref[ATT_WIDTH:, :])
    x1 = _layer_norm(DEEPNORM_ALPHA * xn_ref[...] + mix, g1_ref[...], b1_ref[...])

    qscale = (MEM_HEAD_DIM ** -0.5) * LOG2E
    qm = (_dot(x1.astype(bf16), wq_ref[...]) * qscale).astype(bf16)
    for h in range(MEM_HEADS):
        hs = slice(h * MEM_HEAD_DIM, (h + 1) * MEM_HEAD_DIM)
        logits = _dot_nt(qm[:, hs], kc_ref[0, :, hs])
        p = jnp.exp2(logits - jnp.max(logits, axis=-1, keepdims=True))
        denom = jnp.sum(p, axis=-1, keepdims=True)
        ca = _dot(p.astype(bf16), vc_ref[0, :, hs]) / denom
        ca_ref[:, hs] = ca.astype(bf16)
    x2 = _layer_norm(DEEPNORM_ALPHA * x1 + _dot(ca_ref[...], wom_ref[...]),
                     g2_ref[...], b2_ref[...])

    x2b = x2.astype(bf16)
    down = None
    for c in range(D_FF // FF_CHUNK):
        cs = slice(c * FF_CHUNK, (c + 1) * FF_CHUNK)
        gate = _dot(x2b, wg_ref[:, cs])
        up = _dot(x2b, wu_ref[:, cs])
        hdn = (gate * (1.0 / (1.0 + jnp.exp(-gate))) * up).astype(bf16)
        part = _dot(hdn, wd_ref[cs, :])
        down = part if down is None else down + part
    o_ref[...] = _layer_norm(DEEPNORM_ALPHA * x2 + down, g3_ref[...], b3_ref[...])


def _post(xn, att, c, kc, vc, wo, g1, b1, wq, wom, g2, b2, wg, wu, wd, g3, b3, tiles_per_batch):
    t, d = xn.shape
    nt = t // POST_TM
    row = lambda i: (i, 0)
    const = lambda i: (0, 0)

    def resident(shape):
        return pl.BlockSpec(shape, const, pipeline_mode=pl.Buffered(1))

    return pl.pallas_call(
        _post_kernel,
        grid=(nt,),
        in_specs=[
            pl.BlockSpec((POST_TM, d), row),
            pl.BlockSpec((POST_TM, ATT_WIDTH), row),
            pl.BlockSpec((POST_TM, CONV_WIDTH), row),
            pl.BlockSpec((1, MEM_LEN, d), lambda i: (i // tiles_per_batch, 0, 0)),
            pl.BlockSpec((1, MEM_LEN, d), lambda i: (i // tiles_per_batch, 0, 0)),
            resident((d, d)), resident((1, d)), resident((1, d)),
            resident((d, d)), resident((d, d)), resident((1, d)), resident((1, d)),
            resident((d, D_FF)), resident((d, D_FF)), resident((D_FF, d)),
            resident((1, d)), resident((1, d)),
        ],
        out_specs=pl.BlockSpec((POST_TM, d), row),
        out_shape=jax.ShapeDtypeStruct((t, d), jnp.float32),
        scratch_shapes=[pltpu.VMEM((POST_TM, d), jnp.bfloat16)],
        compiler_params=pltpu.CompilerParams(
            dimension_semantics=("arbitrary",), vmem_limit_bytes=VMEM_LIMIT),
        name="post",
    )(xn, att, c, kc, vc, wo, g1, b1, wq, wom, g2, b2, wg, wu, wd, g3, b3)


def kernel(x, mem, in_norm_g, in_norm_b, w_in, lambda_q1, lambda_k1, lambda_q2, lambda_k2,
           subln_g, conv_w, conv_b, conv_norm_g, conv_norm_b, w_pw, b_pw, w_o, ln1_g, ln1_b,
           w_q_mem, w_kv_mem, w_o_mem, ln2_g, ln2_b, w_gate, w_up, w_down, ln3_g, ln3_b):
    bsz, s, d = x.shape
    bf16 = jnp.bfloat16
    row = lambda v: v.reshape(1, -1)
    slopes2 = jnp.asarray([2.0 ** (-8.0 * (i + 1) / ATT_HEADS) * LOG2E for i in range(ATT_HEADS)],
                          jnp.float32)
    g_in, b_in = row(in_norm_g), row(in_norm_b)
    for l in range(DEPTH):
        lam_init = 0.8 - 0.6 * math.exp(-0.3 * l)
        w = w_in[l].astype(bf16)
        wqT = w[:, :QK_TOTAL].T
        wk = w[:, QK_TOTAL:2 * QK_TOTAL]
        wvT = w[:, 2 * QK_TOTAL:2 * QK_TOTAL + ATT_WIDTH].T
        wc = w[:, 2 * QK_TOTAL + ATT_WIDTH:]
        xn, qT, k, vT, glu = _in_proj(x, g_in, b_in, wqT, wk, wvT, wc)
        att = _diff_attention(slopes2, qT, k, vT, row(lambda_q1[l]), row(lambda_k1[l]),
                              row(lambda_q2[l]), row(lambda_k2[l]),
                              subln_g[l].reshape(-1, 1), lam_init)
        c = _conv_module(glu, conv_w[l], row(conv_b[l]), row(conv_norm_g[l]),
                         row(conv_norm_b[l]), w_pw[l].astype(bf16), row(b_pw[l]))
        kc, vc = _mem_kv(mem, w_kv_mem[l].astype(bf16))
        out = _post(xn.reshape(bsz * s, d), att.reshape(bsz * s, ATT_WIDTH),
                    c.reshape(bsz * s, CONV_WIDTH), kc, vc,
                    w_o[l].astype(bf16), row(ln1_g[l]), row(ln1_b[l]),
                    w_q_mem[l].astype(bf16), w_o_mem[l].astype(bf16), row(ln2_g[l]), row(ln2_b[l]),
                    w_gate[l].astype(bf16), w_up[l].astype(bf16), w_down[l].astype(bf16),
                    row(ln3_g[l]), row(ln3_b[l]), s // POST_TM)
        x = out.reshape(bsz, s, d)
        g_in = b_in = None
    return x
```

```python
import functools
import math

import jax
import jax.numpy as jnp
from jax import lax
from jax.experimental import pallas as pl
from jax.experimental.pallas import tpu as pltpu

D_MODEL = 1024
DEPTH = 1
MEM_LEN = 256
ATT_WIDTH = 512
CONV_WIDTH = 512
ATT_HEADS = 4
ATT_HEAD_DIM = 128
QK_DIM = 64
QK_TOTAL = 512
CONV_GROUPS = 4
CONV_GROUP_DIM = 128
CONV_KERNEL = 31
MEM_HEADS = 4
MEM_HEAD_DIM = 256
D_FF = 2816
LN_EPS = 1e-5
DEEPNORM_ALPHA = (2 * DEPTH) ** 0.25

LOG2E = 1.4426950408889634
NEG_BIG = -1e30

ATT_BQ = 512
ATT_BK = 512
ATT_SUM_ROWS = 16
PROJ_TM = 512
CONV_HALO = 32
CONV_ROWS = 64
CONV_CHUNKS = 2
POST_TM = 512
FF_CHUNK = 1408

VMEM_LIMIT = 60 * 1024 * 1024

_NT = (((1,), (1,)), ((), ()))


def _layer_norm(x, g, b):
    mu = jnp.mean(x, axis=-1, keepdims=True)
    xc = x - mu
    var = jnp.mean(xc * xc, axis=-1, keepdims=True)
    return xc * lax.rsqrt(var + LN_EPS) * g + b


def _alibi_slope(h):
    return 2.0 ** (-8.0 * (h + 1) / ATT_HEADS)


def _dot(a, b):
    return jnp.dot(a, b, preferred_element_type=jnp.float32)


def _dot_nt(a, b):
    return lax.dot_general(a, b, _NT, preferred_element_type=jnp.float32)


def _inproj_kernel(x_ref, g_ref, b_ref, wq_ref, wk_ref, wv_ref, wc_ref,
                   cw_ref, cb_ref, ng_ref, nb_ref, wpw_ref, bpw_ref,
                   xn_ref, qT_ref, k_ref, vT_ref, c_ref, xx_ref, act_ref):
    si = pl.program_id(1)
    tm = PROJ_TM
    xn = _layer_norm(x_ref[0], g_ref[...], b_ref[...])
    xn_ref[0] = xn
    xb = xn.astype(jnp.bfloat16)

    c_val = _dot(xb, wc_ref[:, :CONV_WIDTH])
    c_gate = _dot(xb, wc_ref[:, CONV_WIDTH:])
    glu = c_val * (1.0 / (1.0 + jnp.exp(-c_gate)))

    @pl.when(si == 0)
    def _():
        xx_ref[:, :CONV_HALO, :] = jnp.zeros((CONV_GROUPS, CONV_HALO, CONV_GROUP_DIM), jnp.float32)

    for g in range(CONV_GROUPS):
        xx_ref[g, CONV_HALO:, :] = glu[:, g * CONV_GROUP_DIM:(g + 1) * CONV_GROUP_DIM]

    qscale = (QK_DIM ** -0.5) * LOG2E
    half = QK_TOTAL // 2

    def k_piece(i):
        cs = slice(i * half, (i + 1) * half)
        k_ref[0, :, cs] = _dot(xb, wk_ref[:, cs]).astype(jnp.bfloat16)

    def q_piece(i):
        qT = (_dot(xb, wq_ref[:, i * half:(i + 1) * half]) * qscale).T
        for j in range(tm // ATT_BQ):
            qT_ref[0, j, i * half:(i + 1) * half, :] = qT[:, j * ATT_BQ:(j + 1) * ATT_BQ].astype(jnp.bfloat16)

    def v_piece(i):
        vT = _dot(xb, wv_ref[:, i * half:(i + 1) * half]).T
        for j in range(tm // ATT_BK):
            vT_ref[0, j, i * half:(i + 1) * half, :] = vT[:, j * ATT_BK:(j + 1) * ATT_BK].astype(jnp.bfloat16)

    pieces = [functools.partial(f, i) for f in (k_piece, q_piece, v_piece) for i in range(2)]

    first = CONV_HALO - (CONV_KERNEL - 1)
    for g in range(CONV_GROUPS):
        gs = slice(g * CONV_GROUP_DIM, (g + 1) * CONV_GROUP_DIM)
        for r0 in range(0, tm, CONV_ROWS * CONV_CHUNKS):
            if pieces:
                pieces.pop(0)()
            accs = [jnp.broadcast_to(cb_ref[:, gs], (CONV_ROWS, CONV_GROUP_DIM))] * CONV_CHUNKS
            for j in range(CONV_KERNEL):
                wj = jnp.broadcast_to(cw_ref[j:j + 1, gs], (CONV_ROWS, CONV_GROUP_DIM))
                for rc in range(CONV_CHUNKS):
                    r = r0 + rc * CONV_ROWS + first + j
                    accs[rc] = accs[rc] + wj * xx_ref[g, r:r + CONV_ROWS, :]
            for rc in range(CONV_CHUNKS):
                y = _layer_norm(accs[rc], ng_ref[:, gs], nb_ref[:, gs])
                y = y * (1.0 / (1.0 + jnp.exp(-y)))
                r = r0 + rc * CONV_ROWS
                act_ref[r:r + CONV_ROWS, gs] = y.astype(jnp.bfloat16)
        xx_ref[g, :CONV_HALO, :] = xx_ref[g, tm:tm + CONV_HALO, :]
    assert not pieces
    c_ref[0] = (_dot(act_ref[...], wpw_ref[...]) + bpw_ref[...]).astype(c_ref.dtype)


def _in_proj(x, g, b, wq, wk, wv, wc, cw, cb, ng, nb, wpw, bpw):
    bsz, s, d = x.shape
    nt = s // PROJ_TM
    nbq = PROJ_TM // ATT_BQ
    nbk = PROJ_TM // ATT_BK
    cw_ = CONV_WIDTH
    const = lambda bi, si: (0, 0)
    return pl.pallas_call(
        _inproj_kernel,
        grid=(bsz, nt),
        in_specs=[
            pl.BlockSpec((1, PROJ_TM, d), lambda bi, si: (bi, si, 0)),
            pl.BlockSpec((1, d), const),
            pl.BlockSpec((1, d), const),
            pl.BlockSpec((d, QK_TOTAL), const),
            pl.BlockSpec((d, QK_TOTAL), const),
            pl.BlockSpec((d, ATT_WIDTH), const),
            pl.BlockSpec((d, 2 * cw_), const),
            pl.BlockSpec((CONV_KERNEL, cw_), const),
            pl.BlockSpec((1, cw_), const),
            pl.BlockSpec((1, cw_), const),
            pl.BlockSpec((1, cw_), const),
            pl.BlockSpec((cw_, cw_), const),
            pl.BlockSpec((1, cw_), const),
        ],
        out_specs=[
            pl.BlockSpec((1, PROJ_TM, d), lambda bi, si: (bi, si, 0)),
            pl.BlockSpec((1, nbq, QK_TOTAL, ATT_BQ), lambda bi, si: (bi, si, 0, 0)),
            pl.BlockSpec((1, PROJ_TM, QK_TOTAL), lambda bi, si: (bi, si, 0)),
            pl.BlockSpec((1, nbk, ATT_WIDTH, ATT_BK), lambda bi, si: (bi, si, 0, 0)),
            pl.BlockSpec((1, PROJ_TM, cw_), lambda bi, si: (bi, si, 0)),
        ],
        out_shape=[
            jax.ShapeDtypeStruct((bsz, s, d), jnp.float32),
            jax.ShapeDtypeStruct((bsz, s // ATT_BQ, QK_TOTAL, ATT_BQ), jnp.bfloat16),
            jax.ShapeDtypeStruct((bsz, s, QK_TOTAL), jnp.bfloat16),
            jax.ShapeDtypeStruct((bsz, s // ATT_BK, ATT_WIDTH, ATT_BK), jnp.bfloat16),
            jax.ShapeDtypeStruct((bsz, s, cw_), jnp.bfloat16),
        ],
        scratch_shapes=[
            pltpu.VMEM((CONV_GROUPS, CONV_HALO + PROJ_TM, CONV_GROUP_DIM), jnp.float32),
            pltpu.VMEM((PROJ_TM, cw_), jnp.bfloat16),
        ],
        compiler_params=pltpu.CompilerParams(
            dimension_semantics=("arbitrary", "arbitrary"),
            vmem_limit_bytes=VMEM_LIMIT),
        name="in_proj",
    )(x, g, b, wq, wk, wv, wc, cw, cb, ng, nb, wpw, bpw)


def _attn_kernel(slope_ref, qT_ref, k_ref, vT_ref, lq1_ref, lk1_ref, lq2_ref, lk2_ref,
                 gain_ref, o_ref, qs_ref, kbias_ref, m_ref, acc_ref,
                 sa_ref, sb_ref, mxa_ref, mxb_ref, *, lam_init):
    bq, bk = ATT_BQ, ATT_BK
    h = pl.program_id(1)
    qi = pl.program_id(2)
    slope2 = slope_ref[h]

    @pl.when(qi == 0)
    def _():
        pos = lax.broadcasted_iota(jnp.int32, (bk, ATT_HEAD_DIM), 0)
        lane = lax.broadcasted_iota(jnp.int32, (bk, ATT_HEAD_DIM), 1)
        u = slope2 * pos.astype(jnp.float32)
        hi = u.astype(jnp.bfloat16).astype(jnp.float32)
        mid = (u - hi).astype(jnp.bfloat16).astype(jnp.float32)
        lo = u - hi - mid
        cols = jnp.where(lane == 0, hi, jnp.where(lane == 1, mid, jnp.where(lane == 2, lo, 0.0)))
        kbias_ref[...] = cols.astype(jnp.bfloat16)

    qT = qT_ref[0, 0]
    row = lax.broadcasted_iota(jnp.int32, (ATT_HEAD_DIM, bq), 0)
    zero = jnp.zeros_like(qT)
    qs_ref[:ATT_HEAD_DIM, :bq] = jnp.where(row < QK_DIM, qT, zero)
    qs_ref[:ATT_HEAD_DIM, bq:] = jnp.where(row >= QK_DIM, qT, zero)
    row2 = lax.broadcasted_iota(jnp.int32, (ATT_HEAD_DIM, 2 * bq), 0)
    qs_ref[ATT_HEAD_DIM:, :] = jnp.where(row2 < 3, 1.0, 0.0).astype(jnp.bfloat16)

    m_ref[...] = jnp.full(m_ref.shape, NEG_BIG, jnp.float32)
    acc_ref[...] = jnp.zeros(acc_ref.shape, jnp.float32)

    q0 = qi * bq
    n = q0 // bk
    ones_rows = jnp.ones((ATT_SUM_ROWS, bk), jnp.bfloat16)

    def produce(j, s_ref, mx_ref, masked):
        start = pl.multiple_of(j * bk, bk)
        kb = jnp.concatenate([k_ref[0, pl.ds(start, bk), :], kbias_ref[...]], axis=1)
        s = _dot(kb, qs_ref[...])
        if masked:
            kr = lax.broadcasted_iota(jnp.int32, (bk, bq), 0)
            qc = lax.broadcasted_iota(jnp.int32, (bk, bq), 1)
            valid = (kr - qc) <= (q0 - j * bk)
            s = jnp.where(jnp.concatenate([valid, valid], axis=1), s, NEG_BIG)
        s_ref[...] = s
        mx_ref[...] = jnp.max(s, axis=0, keepdims=True)

    def consume(j, s_ref, mx_ref):
        off = slope2 * (j * bk - q0).astype(jnp.float32)
        m_old = m_ref[...]
        m_new = jnp.maximum(m_old, mx_ref[...] + off)
        p = jnp.exp2(s_ref[...] - (m_new - off)).astype(jnp.bfloat16)
        alpha = jnp.exp2(m_old - m_new)
        lhs = jnp.concatenate([vT_ref[0, j], ones_rows], axis=0)
        acc_ref[...] = alpha * acc_ref[...] + _dot(lhs, p)
        m_ref[...] = m_new

    bufs = ((sa_ref, mxa_ref), (sb_ref, mxb_ref))

    @pl.when(n == 0)
    def _():
        produce(0, *bufs[0], True)

    @pl.when(n > 0)
    def _():
        produce(0, *bufs[0], False)

    npair = jnp.maximum(n - 1, 0) // 2

    def pair(i, carry):
        j = 2 * i
        produce(j + 1, *bufs[1], False)
        consume(j, *bufs[0])
        produce(j + 2, *bufs[0], False)
        consume(j + 1, *bufs[1])
        return carry

    lax.fori_loop(0, npair, pair, 0)
    j0 = 2 * npair
    rest = n - j0

    @pl.when(rest == 0)
    def _():
        consume(0, *bufs[0])

    @pl.when(rest == 1)
    def _():
        produce(n, *bufs[1], True)
        consume(j0, *bufs[0])
        consume(n, *bufs[1])

    @pl.when(rest == 2)
    def _():
        produce(j0 + 1, *bufs[1], False)
        consume(j0, *bufs[0])
        produce(n, *bufs[0], True)
        consume(j0 + 1, *bufs[1])
        consume(n, *bufs[0])

    lam = (jnp.exp(jnp.sum(lq1_ref[...] * lk1_ref[...], axis=-1, keepdims=True))
           - jnp.exp(jnp.sum(lq2_ref[...] * lk2_ref[...], axis=-1, keepdims=True))
           + lam_init)
    acc = acc_ref[:ATT_HEAD_DIM, :]
    l = acc_ref[ATT_HEAD_DIM:ATT_HEAD_DIM + 1, :]
    oT = acc[:, :bq] / l[:, :bq] - lam * (acc[:, bq:] / l[:, bq:])
    ms = jnp.mean(oT * oT, axis=0, keepdims=True)
    oT = oT * lax.rsqrt(ms + LN_EPS) * gain_ref[...] * (1.0 - lam_init)
    o_ref[0] = oT.T.astype(o_ref.dtype)


def _diff_attention(slopes2, qT, k, vT, lq1, lk1, lq2, lk2, gain_col, lam_init):
    bsz, nq, _, bq = qT.shape
    _, nk, _, bk = vT.shape
    s = nq * bq
    vec = lambda bi, h, qi: (0, 0)
    return pl.pallas_call(
        functools.partial(_attn_kernel, lam_init=lam_init),
        grid=(bsz, ATT_HEADS, nq),
        in_specs=[
            pl.BlockSpec(memory_space=pltpu.SMEM),
            pl.BlockSpec((1, 1, ATT_HEAD_DIM, bq), lambda bi, h, qi: (bi, qi, h, 0)),
            pl.BlockSpec((1, s, ATT_HEAD_DIM), lambda bi, h, qi: (bi, 0, h)),
            pl.BlockSpec((1, nk, ATT_HEAD_DIM, bk), lambda bi, h, qi: (bi, 0, h, 0)),
            pl.BlockSpec((1, QK_DIM), vec),
            pl.BlockSpec((1, QK_DIM), vec),
            pl.BlockSpec((1, QK_DIM), vec),
            pl.BlockSpec((1, QK_DIM), vec),
            pl.BlockSpec((ATT_HEAD_DIM, 1), vec),
        ],
        out_specs=pl.BlockSpec((1, bq, ATT_HEAD_DIM), lambda bi, h, qi: (bi, qi, h)),
        out_shape=jax.ShapeDtypeStruct((bsz, s, ATT_WIDTH), jnp.bfloat16),
        scratch_shapes=[
            pltpu.VMEM((2 * ATT_HEAD_DIM, 2 * bq), jnp.bfloat16),
            pltpu.VMEM((bk, ATT_HEAD_DIM), jnp.bfloat16),
            pltpu.VMEM((1, 2 * bq), jnp.float32),
            pltpu.VMEM((ATT_HEAD_DIM + ATT_SUM_ROWS, 2 * bq), jnp.float32),
            pltpu.VMEM((bk, 2 * bq), jnp.float32),
            pltpu.VMEM((bk, 2 * bq), jnp.float32),
            pltpu.VMEM((1, 2 * bq), jnp.float32),
            pltpu.VMEM((1, 2 * bq), jnp.float32),
        ],
        compiler_params=pltpu.CompilerParams(
            dimension_semantics=("arbitrary", "arbitrary", "arbitrary"),
            vmem_limit_bytes=VMEM_LIMIT),
        name="diff_attn",
    )(slopes2, qT, k, vT, lq1, lk1, lq2, lk2, gain_col)


def _memkv_kernel(mem_ref, w_ref, kc_ref, vc_ref):
    mb = mem_ref[0].astype(jnp.bfloat16)
    kc_ref[0] = _dot(mb, w_ref[:, :D_MODEL]).astype(jnp.bfloat16)
    vc_ref[0] = _dot(mb, w_ref[:, D_MODEL:]).astype(jnp.bfloat16)


def _mem_kv(mem, w_kv):
    bsz, m, d = mem.shape
    return pl.pallas_call(
        _memkv_kernel,
        grid=(bsz,),
        in_specs=[
            pl.BlockSpec((1, m, d), lambda bi: (bi, 0, 0)),
            pl.BlockSpec((d, 2 * d), lambda bi: (0, 0)),
        ],
        out_specs=[
            pl.BlockSpec((1, m, d), lambda bi: (bi, 0, 0)),
            pl.BlockSpec((1, m, d), lambda bi: (bi, 0, 0)),
        ],
        out_shape=[
            jax.ShapeDtypeStruct((bsz, m, d), jnp.bfloat16),
            jax.ShapeDtypeStruct((bsz, m, d), jnp.bfloat16),
        ],
        compiler_params=pltpu.CompilerParams(
            dimension_semantics=("arbitrary",), vmem_limit_bytes=VMEM_LIMIT),
        name="mem_kv",
    )(mem, w_kv)


def _post_kernel(xn_ref, att_ref, c_ref, kc_ref, vc_ref,
                 wo_ref, g1_ref, b1_ref, wq_ref, wom_ref, g2_ref, b2_ref,
                 wg_ref, wu_ref, wd_ref, g3_ref, b3_ref, o_ref, ca_ref):
    bf16 = jnp.bfloat16
    mix = _dot(att_ref[...], wo_ref[:ATT_WIDTH, :]) + _dot(c_ref[...], wo_ref[ATT_WIDTH:, :])
    x1 = _layer_norm(DEEPNORM_ALPHA * xn_ref[...] + mix, g1_ref[...], b1_ref[...])

    qscale = (MEM_HEAD_DIM ** -0.5) * LOG2E
    qm = (_dot(x1.astype(bf16), wq_ref[...]) * qscale).astype(bf16)
    for h in range(MEM_HEADS):
        hs = slice(h * MEM_HEAD_DIM, (h + 1) * MEM_HEAD_DIM)
        logits = _dot_nt(qm[:, hs], kc_ref[0, :, hs])
        p = jnp.exp2(logits - jnp.max(logits, axis=-1, keepdims=True))
        denom = jnp.sum(p, axis=-1, keepdims=True)
        ca = _dot(p.astype(bf16), vc_ref[0, :, hs]) / denom
        ca_ref[:, hs] = ca.astype(bf16)
    x2 = _layer_norm(DEEPNORM_ALPHA * x1 + _dot(ca_ref[...], wom_ref[...]),
                     g2_ref[...], b2_ref[...])

    x2b = x2.astype(bf16)
    down = None
    for c in range(D_FF // FF_CHUNK):
        cs = slice(c * FF_CHUNK, (c + 1) * FF_CHUNK)
        gate = _dot(x2b, wg_ref[:, cs])
        up = _dot(x2b, wu_ref[:, cs])
        hdn = (gate * (1.0 / (1.0 + jnp.exp(-gate))) * up).astype(bf16)
        part = _dot(hdn, wd_ref[cs, :])
        down = part if down is None else down + part
    o_ref[...] = _layer_norm(DEEPNORM_ALPHA * x2 + down, g3_ref[...], b3_ref[...])


def _post(xn, att, c, kc, vc, wo, g1, b1, wq, wom, g2, b2, wg, wu, wd, g3, b3, tiles_per_batch):
    t, d = xn.shape
    nt = t // POST_TM
    row = lambda i: (i, 0)
    const = lambda i: (0, 0)

    def resident(shape):
        return pl.BlockSpec(shape, const, pipeline_mode=pl.Buffered(1))

    return pl.pallas_call(
        _post_kernel,
        grid=(nt,),
        in_specs=[
            pl.BlockSpec((POST_TM, d), row),
            pl.BlockSpec((POST_TM, ATT_WIDTH), row),
            pl.BlockSpec((POST_TM, CONV_WIDTH), row),
            pl.BlockSpec((1, MEM_LEN, d), lambda i: (i // tiles_per_batch, 0, 0)),
            pl.BlockSpec((1, MEM_LEN, d), lambda i: (i // tiles_per_batch, 0, 0)),
            resident((d, d)), resident((1, d)), resident((1, d)),
            resident((d, d)), resident((d, d)), resident((1, d)), resident((1, d)),
            resident((d, D_FF)), resident((d, D_FF)), resident((D_FF, d)),
            resident((1, d)), resident((1, d)),
        ],
        out_specs=pl.BlockSpec((POST_TM, d), row),
        out_shape=jax.ShapeDtypeStruct((t, d), jnp.float32),
        scratch_shapes=[pltpu.VMEM((POST_TM, d), jnp.bfloat16)],
        compiler_params=pltpu.CompilerParams(
            dimension_semantics=("arbitrary",), vmem_limit_bytes=VMEM_LIMIT),
        name="post",
    )(xn, att, c, kc, vc, wo, g1, b1, wq, wom, g2, b2, wg, wu, wd, g3, b3)


def kernel(x, mem, in_norm_g, in_norm_b, w_in, lambda_q1, lambda_k1, lambda_q2, lambda_k2,
           subln_g, conv_w, conv_b, conv_norm_g, conv_norm_b, w_pw, b_pw, w_o, ln1_g, ln1_b,
           w_q_mem, w_kv_mem, w_o_mem, ln2_g, ln2_b, w_gate, w_up, w_down, ln3_g, ln3_b):
    assert DEPTH == 1 and w_in.shape[0] == 1
    bsz, s, d = x.shape
    bf16 = jnp.bfloat16
    row = lambda v: v.reshape(1, -1)
    l = 0
    lam_init = 0.8 - 0.6 * math.exp(-0.3 * l)
    slopes2 = jnp.asarray([_alibi_slope(h) * LOG2E for h in range(ATT_HEADS)], jnp.float32)
    w = w_in[l].astype(bf16)
    wq = w[:, :QK_TOTAL]
    wk = w[:, QK_TOTAL:2 * QK_TOTAL]
    wv = w[:, 2 * QK_TOTAL:2 * QK_TOTAL + ATT_WIDTH]
    wc = w[:, 2 * QK_TOTAL + ATT_WIDTH:]
    xn, qT, k, vT, c = _in_proj(x, row(in_norm_g), row(in_norm_b), wq, wk, wv, wc,
                                conv_w[l], row(conv_b[l]), row(conv_norm_g[l]),
                                row(conv_norm_b[l]), w_pw[l].astype(bf16), row(b_pw[l]))
    att = _diff_attention(slopes2, qT, k, vT, row(lambda_q1[l]), row(lambda_k1[l]),
                          row(lambda_q2[l]), row(lambda_k2[l]),
                          subln_g[l].reshape(-1, 1), lam_init)
    kc, vc = _mem_kv(mem, w_kv_mem[l].astype(bf16))
    out = _post(xn.reshape(bsz * s, d), att.reshape(bsz * s, ATT_WIDTH),
                c.reshape(bsz * s, CONV_WIDTH), kc, vc,
                w_o[l].astype(bf16), row(ln1_g[l]), row(ln1_b[l]),
                w_q_mem[l].astype(bf16), w_o_mem[l].astype(bf16), row(ln2_g[l]), row(ln2_b[l]),
                w_gate[l].astype(bf16), w_up[l].astype(bf16), w_down[l].astype(bf16),
                row(ln3_g[l]), row(ln3_b[l]), s // POST_TM)
    return out.reshape(bsz, s, d)
```

```python
import functools
import math

import jax
import jax.numpy as jnp
from jax import lax
from jax.experimental import pallas as pl
from jax.experimental.pallas import tpu as pltpu

D_MODEL = 1024
DEPTH = 1
MEM_LEN = 256
ATT_WIDTH = 512
CONV_WIDTH = 512
ATT_HEADS = 4
ATT_HEAD_DIM = 128
QK_DIM = 64
QK_TOTAL = 512
CONV_GROUPS = 4
CONV_GROUP_DIM = 128
CONV_KERNEL = 31
MEM_HEADS = 4
MEM_HEAD_DIM = 256
D_FF = 2816
LN_EPS = 1e-5
DEEPNORM_ALPHA = (2 * DEPTH) ** 0.25

LOG2E = 1.4426950408889634
NEG_BIG = -1e30

ATT_BQ = 512
ATT_BK = 512
ATT_SUM_ROWS = 16
PROJ_TM = 512
CONV_HALO = 32
CONV_ROWS = 64
CONV_CHUNKS = 2
POST_TM = 512
FF_CHUNK = 1408

VMEM_LIMIT = 60 * 1024 * 1024

_NT = (((1,), (1,)), ((), ()))


def _layer_norm(x, g, b):
    mu = jnp.mean(x, axis=-1, keepdims=True)
    xc = x - mu
    var = jnp.mean(xc * xc, axis=-1, keepdims=True)
    return xc * lax.rsqrt(var + LN_EPS) * g + b


def _alibi_slope(h):
    return 2.0 ** (-8.0 * (h + 1) / ATT_HEADS)


def _dot(a, b):
    return jnp.dot(a, b, preferred_element_type=jnp.float32)


def _dot_nt(a, b):
    return lax.dot_general(a, b, _NT, preferred_element_type=jnp.float32)


def _inproj_kernel(x_ref, g_ref, b_ref, wq_ref, wk_ref, wv_ref, wc_ref,
                   xn_ref, qT_ref, k_ref, vT_ref, glu_ref):
    tm = PROJ_TM
    xn = _layer_norm(x_ref[0], g_ref[...], b_ref[...])
    xn_ref[0] = xn
    xb = xn.astype(jnp.bfloat16)
    k_ref[0] = _dot(xb, wk_ref[...]).astype(jnp.bfloat16)
    qscale = (QK_DIM ** -0.5) * LOG2E
    qT = (_dot(xb, wq_ref[...]) * qscale).T
    vT = _dot(xb, wv_ref[...]).T
    for j in range(tm // ATT_BQ):
        qT_ref[0, j] = qT[:, j * ATT_BQ:(j + 1) * ATT_BQ].astype(jnp.bfloat16)
    for j in range(tm // ATT_BK):
        vT_ref[0, j] = vT[:, j * ATT_BK:(j + 1) * ATT_BK].astype(jnp.bfloat16)
    c_val = _dot(xb, wc_ref[:, :CONV_WIDTH])
    c_gate = _dot(xb, wc_ref[:, CONV_WIDTH:])
    glu_ref[0] = c_val * (1.0 / (1.0 + jnp.exp(-c_gate)))


def _in_proj(x, g, b, wq, wk, wv, wc):
    bsz, s, d = x.shape
    nt = s // PROJ_TM
    nbq = PROJ_TM // ATT_BQ
    nbk = PROJ_TM // ATT_BK
    const = lambda bi, si: (0, 0)
    return pl.pallas_call(
        _inproj_kernel,
        grid=(bsz, nt),
        in_specs=[
            pl.BlockSpec((1, PROJ_TM, d), lambda bi, si: (bi, si, 0)),
            pl.BlockSpec((1, d), const),
            pl.BlockSpec((1, d), const),
            pl.BlockSpec((d, QK_TOTAL), const),
            pl.BlockSpec((d, QK_TOTAL), const),
            pl.BlockSpec((d, ATT_WIDTH), const),
            pl.BlockSpec((d, 2 * CONV_WIDTH), const),
        ],
        out_specs=[
            pl.BlockSpec((1, PROJ_TM, d), lambda bi, si: (bi, si, 0)),
            pl.BlockSpec((1, nbq, QK_TOTAL, ATT_BQ), lambda bi, si: (bi, si, 0, 0)),
            pl.BlockSpec((1, PROJ_TM, QK_TOTAL), lambda bi, si: (bi, si, 0)),
            pl.BlockSpec((1, nbk, ATT_WIDTH, ATT_BK), lambda bi, si: (bi, si, 0, 0)),
            pl.BlockSpec((1, PROJ_TM, CONV_WIDTH), lambda bi, si: (bi, si, 0)),
        ],
        out_shape=[
            jax.ShapeDtypeStruct((bsz, s, d), jnp.float32),
            jax.ShapeDtypeStruct((bsz, s // ATT_BQ, QK_TOTAL, ATT_BQ), jnp.bfloat16),
            jax.ShapeDtypeStruct((bsz, s, QK_TOTAL), jnp.bfloat16),
            jax.ShapeDtypeStruct((bsz, s // ATT_BK, ATT_WIDTH, ATT_BK), jnp.bfloat16),
            jax.ShapeDtypeStruct((bsz, s, CONV_WIDTH), jnp.float32),
        ],
        compiler_params=pltpu.CompilerParams(
            dimension_semantics=("arbitrary", "arbitrary"),
            vmem_limit_bytes=VMEM_LIMIT),
        name="in_proj",
    )(x, g, b, wq, wk, wv, wc)


def _attn_kernel(slope_ref, qT_ref, k_ref, vT_ref, lq1_ref, lk1_ref, lq2_ref, lk2_ref,
                 gain_ref, o_ref, qs_ref, qsn_ref, kbias_ref, m_ref, acc_ref,
                 sa_ref, sb_ref, sc_ref, mxa_ref, mxb_ref, mxc_ref, *, lam_init):
    blk = ATT_BQ
    nq = qT_ref.shape[1]
    slope2 = slope_ref[pl.program_id(1)]

    pos = lax.broadcasted_iota(jnp.int32, (blk, ATT_HEAD_DIM), 0)
    lane = lax.broadcasted_iota(jnp.int32, (blk, ATT_HEAD_DIM), 1)
    u = slope2 * pos.astype(jnp.float32)
    hi = u.astype(jnp.bfloat16).astype(jnp.float32)
    mid = (u - hi).astype(jnp.bfloat16).astype(jnp.float32)
    lo = u - hi - mid
    cols = jnp.where(lane == 0, hi, jnp.where(lane == 1, mid, jnp.where(lane == 2, lo, 0.0)))
    kbias_ref[...] = cols.astype(jnp.bfloat16)

    row2 = lax.broadcasted_iota(jnp.int32, (ATT_HEAD_DIM, 2 * blk), 0)
    ones3 = jnp.where(row2 < 3, 1.0, 0.0).astype(jnp.bfloat16)
    qs_ref[ATT_HEAD_DIM:, :] = ones3
    qsn_ref[ATT_HEAD_DIM:, :] = ones3

    def stack_q(dst_ref, qidx):
        qT = qT_ref[0, qidx]
        row = lax.broadcasted_iota(jnp.int32, (ATT_HEAD_DIM, blk), 0)
        zero = jnp.zeros_like(qT)
        dst_ref[:ATT_HEAD_DIM, :blk] = jnp.where(row < QK_DIM, qT, zero)
        dst_ref[:ATT_HEAD_DIM, blk:] = jnp.where(row >= QK_DIM, qT, zero)

    ones_rows = jnp.ones((ATT_SUM_ROWS, blk), jnp.bfloat16)

    def produce(q_ref, qi, j, buf, masked):
        s_ref, mx_ref = buf
        start = j * blk if isinstance(j, int) else pl.multiple_of(j * blk, blk)
        kb = jnp.concatenate([k_ref[0, pl.ds(start, blk), :], kbias_ref[...]], axis=1)
        s = _dot(kb, q_ref[...])
        if masked:
            kr = lax.broadcasted_iota(jnp.int32, (blk, blk), 0)
            qc = lax.broadcasted_iota(jnp.int32, (blk, blk), 1)
            valid = (kr - qc) <= (qi - j) * blk
            s = jnp.where(jnp.concatenate([valid, valid], axis=1), s, NEG_BIG)
        s_ref[...] = s
        mx_ref[...] = jnp.max(s, axis=0, keepdims=True)

    def consume(qi, j, buf):
        s_ref, mx_ref = buf
        off = slope2 * ((j - qi) * blk).astype(jnp.float32)
        m_old = m_ref[...]
        m_new = jnp.maximum(m_old, mx_ref[...] + off)
        p = jnp.exp2(s_ref[...] - (m_new - off)).astype(jnp.bfloat16)
        alpha = jnp.exp2(m_old - m_new)
        lhs = jnp.concatenate([vT_ref[0, j], ones_rows], axis=0)
        acc_ref[...] = alpha * acc_ref[...] + _dot(lhs, p)
        m_ref[...] = m_new

    lam = (jnp.exp(jnp.sum(lq1_ref[...] * lk1_ref[...], axis=-1, keepdims=True))
           - jnp.exp(jnp.sum(lq2_ref[...] * lk2_ref[...], axis=-1, keepdims=True))
           + lam_init)

    buf_a, buf_b, buf_c = (sa_ref, mxa_ref), (sb_ref, mxb_ref), (sc_ref, mxc_ref)

    stack_q(qsn_ref, 0)
    produce(qsn_ref, 0, 0, buf_c, True)

    def qblock(qi, carry):
        qnext = jnp.minimum(qi + 1, nq - 1)
        stack_q(qs_ref, qi)
        stack_q(qsn_ref, qnext)
        m_ref[...] = jnp.full(m_ref.shape, NEG_BIG, jnp.float32)
        acc_ref[...] = jnp.zeros(acc_ref.shape, jnp.float32)

        def produce_next():
            produce(qsn_ref, qnext, 0, buf_c, False)

        @pl.when(qi == 0)
        def _():
            consume(qi, 0, buf_c)
            produce_next()

        @pl.when(qi == 1)
        def _():
            produce(qs_ref, qi, 1, buf_a, True)
            consume(qi, 0, buf_c)
            produce_next()
            consume(qi, 1, buf_a)

        @pl.when(qi >= 2)
        def _():
            produce(qs_ref, qi, 1, buf_a, False)
            consume(qi, 0, buf_c)

        npair = jnp.maximum(qi - 2, 0) // 2

        def pair(i, c):
            j = 2 * i + 1
            produce(qs_ref, qi, j + 1, buf_b, False)
            consume(qi, j, buf_a)
            produce(qs_ref, qi, j + 2, buf_a, False)
            consume(qi, j + 1, buf_b)
            return c

        lax.fori_loop(0, npair, pair, 0)
        j0 = 2 * npair + 1
        rest = qi - j0

        @pl.when(jnp.logical_and(qi >= 2, rest == 1))
        def _():
            produce(qs_ref, qi, qi, buf_b, True)
            consume(qi, j0, buf_a)
            produce_next()
            consume(qi, qi, buf_b)

        @pl.when(jnp.logical_and(qi >= 2, rest == 2))
        def _():
            produce(qs_ref, qi, j0 + 1, buf_b, False)
            consume(qi, j0, buf_a)
            produce(qs_ref, qi, qi, buf_a, True)
            consume(qi, j0 + 1, buf_b)
            produce_next()
            consume(qi, qi, buf_a)

        acc = acc_ref[:ATT_HEAD_DIM, :]
        l = acc_ref[ATT_HEAD_DIM:ATT_HEAD_DIM + 1, :]
        oT = acc[:, :blk] / l[:, :blk] - lam * (acc[:, blk:] / l[:, blk:])
        ms = jnp.mean(oT * oT, axis=0, keepdims=True)
        oT = oT * lax.rsqrt(ms + LN_EPS) * gain_ref[...] * (1.0 - lam_init)
        o_ref[0, pl.ds(pl.multiple_of(qi * blk, blk), blk), :] = oT.T.astype(o_ref.dtype)
        return carry

    lax.fori_loop(0, nq, qblock, 0)


def _diff_attention(slopes2, qT, k, vT, lq1, lk1, lq2, lk2, gain_col, lam_init):
    bsz, nq, _, bq = qT.shape
    _, nk, _, bk = vT.shape
    assert bq == bk and nq == nk
    s = nq * bq
    vec = lambda bi, h: (0, 0)
    head = lambda bi, h: (bi, 0, h, 0)
    return pl.pallas_call(
        functools.partial(_attn_kernel, lam_init=lam_init),
        grid=(bsz, ATT_HEADS),
        in_specs=[
            pl.BlockSpec(memory_space=pltpu.SMEM),
            pl.BlockSpec((1, nq, ATT_HEAD_DIM, bq), head),
            pl.BlockSpec((1, s, ATT_HEAD_DIM), lambda bi, h: (bi, 0, h)),
            pl.BlockSpec((1, nk, ATT_HEAD_DIM, bk), head),
            pl.BlockSpec((1, QK_DIM), vec),
            pl.BlockSpec((1, QK_DIM), vec),
            pl.BlockSpec((1, QK_DIM), vec),
            pl.BlockSpec((1, QK_DIM), vec),
            pl.BlockSpec((ATT_HEAD_DIM, 1), vec),
        ],
        out_specs=pl.BlockSpec((1, s, ATT_HEAD_DIM), lambda bi, h: (bi, 0, h)),
        out_shape=jax.ShapeDtypeStruct((bsz, s, ATT_WIDTH), jnp.bfloat16),
        scratch_shapes=[
            pltpu.VMEM((2 * ATT_HEAD_DIM, 2 * bq), jnp.bfloat16),
            pltpu.VMEM((2 * ATT_HEAD_DIM, 2 * bq), jnp.bfloat16),
            pltpu.VMEM((bk, ATT_HEAD_DIM), jnp.bfloat16),
            pltpu.VMEM((1, 2 * bq), jnp.float32),
            pltpu.VMEM((ATT_HEAD_DIM + ATT_SUM_ROWS, 2 * bq), jnp.float32),
            pltpu.VMEM((bk, 2 * bq), jnp.float32),
            pltpu.VMEM((bk, 2 * bq), jnp.float32),
            pltpu.VMEM((bk, 2 * bq), jnp.float32),
            pltpu.VMEM((1, 2 * bq), jnp.float32),
            pltpu.VMEM((1, 2 * bq), jnp.float32),
            pltpu.VMEM((1, 2 * bq), jnp.float32),
        ],
        compiler_params=pltpu.CompilerParams(
            dimension_semantics=("arbitrary", "arbitrary"),
            vmem_limit_bytes=VMEM_LIMIT),
        name="diff_attn",
    )(slopes2, qT, k, vT, lq1, lk1, lq2, lk2, gain_col)


def _memkv_kernel(mem_ref, w_ref, kc_ref, vc_ref):
    mb = mem_ref[0].astype(jnp.bfloat16)
    kc_ref[0] = _dot(mb, w_ref[:, :D_MODEL]).astype(jnp.bfloat16)
    vc_ref[0] = _dot(mb, w_ref[:, D_MODEL:]).astype(jnp.bfloat16)


def _mem_kv(mem, w_kv):
    bsz, m, d = mem.shape
    return pl.pallas_call(
        _memkv_kernel,
        grid=(bsz,),
        in_specs=[
            pl.BlockSpec((1, m, d), lambda bi: (bi, 0, 0)),
            pl.BlockSpec((d, 2 * d), lambda bi: (0, 0)),
        ],
        out_specs=[
            pl.BlockSpec((1, m, d), lambda bi: (bi, 0, 0)),
            pl.BlockSpec((1, m, d), lambda bi: (bi, 0, 0)),
        ],
        out_shape=[
            jax.ShapeDtypeStruct((bsz, m, d), jnp.bfloat16),
            jax.ShapeDtypeStruct((bsz, m, d), jnp.bfloat16),
        ],
        compiler_params=pltpu.CompilerParams(
            dimension_semantics=("arbitrary",), vmem_limit_bytes=VMEM_LIMIT),
        name="mem_kv",
    )(mem, w_kv)


def _conv_taps(glu_ref, halo, cw_ref, cb_ref, ng_ref, nb_ref, xx_ref, act_ref):
    tm = POST_TM
    first = CONV_HALO - (CONV_KERNEL - 1)
    for g in range(CONV_GROUPS):
        gs = slice(g * CONV_GROUP_DIM, (g + 1) * CONV_GROUP_DIM)
        xx_ref[g, :CONV_HALO, :] = halo[:, gs]
        xx_ref[g, CONV_HALO:, :] = glu_ref[:, gs]
    for g in range(CONV_GROUPS):
        gs = slice(g * CONV_GROUP_DIM, (g + 1) * CONV_GROUP_DIM)
        for r0 in range(0, tm, CONV_ROWS * CONV_CHUNKS):
            accs = [jnp.broadcast_to(cb_ref[:, gs], (CONV_ROWS, CONV_GROUP_DIM))] * CONV_CHUNKS
            for j in range(CONV_KERNEL):
                wj = jnp.broadcast_to(cw_ref[j:j + 1, gs], (CONV_ROWS, CONV_GROUP_DIM))
                for rc in range(CONV_CHUNKS):
                    r = r0 + rc * CONV_ROWS + first + j
                    accs[rc] = accs[rc] + wj * xx_ref[g, r:r + CONV_ROWS, :]
            for rc in range(CONV_CHUNKS):
                y = _layer_norm(accs[rc], ng_ref[:, gs], nb_ref[:, gs])
                y = y * (1.0 / (1.0 + jnp.exp(-y)))
                r = r0 + rc * CONV_ROWS
                act_ref[r:r + CONV_ROWS, gs] = y.astype(jnp.bfloat16)


def _post_kernel(xn_ref, att_ref, glu0_ref, glun_ref, halo_ref, kc_ref, vc_ref,
                 cw_ref, cb_ref, ng_ref, nb_ref, wpw_ref, bpw_ref,
                 wo_ref, g1_ref, b1_ref, wq_ref, wom_ref, g2_ref, b2_ref,
                 wg_ref, wu_ref, wd_ref, g3_ref, b3_ref, o_ref,
                 ca_ref, c_ref, xx_ref, act_ref, *, tiles_per_batch):
    bf16 = jnp.bfloat16
    i = pl.program_id(0)

    def pointwise():
        return (_dot(act_ref[...], wpw_ref[...]) + bpw_ref[...]).astype(bf16)

    @pl.when(i == 0)
    def _():
        _conv_taps(glu0_ref, jnp.zeros((CONV_HALO, CONV_WIDTH), jnp.float32),
                   cw_ref, cb_ref, ng_ref, nb_ref, xx_ref, act_ref)
        c_ref[...] = pointwise()

    mix = _dot(att_ref[...], wo_ref[:ATT_WIDTH, :]) + _dot(c_ref[...], wo_ref[ATT_WIDTH:, :])
    x1 = _layer_norm(DEEPNORM_ALPHA * xn_ref[...] + mix, g1_ref[...], b1_ref[...])

    halo = halo_ref[...]
    halo = jnp.where((i + 1) % tiles_per_batch == 0, jnp.zeros_like(halo), halo)
    _conv_taps(glun_ref, halo, cw_ref, cb_ref, ng_ref, nb_ref, xx_ref, act_ref)

    qscale = (MEM_HEAD_DIM ** -0.5) * LOG2E
    qm = (_dot(x1.astype(bf16), wq_ref[...]) * qscale).astype(bf16)
    for h in range(MEM_HEADS):
        hs = slice(h * MEM_HEAD_DIM, (h + 1) * MEM_HEAD_DIM)
        logits = _dot_nt(qm[:, hs], kc_ref[0, :, hs])
        p = jnp.exp2(logits - jnp.max(logits, axis=-1, keepdims=True))
        denom = jnp.sum(p, axis=-1, keepdims=True)
        ca = _dot(p.astype(bf16), vc_ref[0, :, hs]) / denom
        ca_ref[:, hs] = ca.astype(bf16)
    x2 = _layer_norm(DEEPNORM_ALPHA * x1 + _dot(ca_ref[...], wom_ref[...]),
                     g2_ref[...], b2_ref[...])

    x2b = x2.astype(bf16)
    down = None
    for c in range(D_FF // FF_CHUNK):
        cs = slice(c * FF_CHUNK, (c + 1) * FF_CHUNK)
        gate = _dot(x2b, wg_ref[:, cs])
        up = _dot(x2b, wu_ref[:, cs])
        hdn = (gate * (1.0 / (1.0 + jnp.exp(-gate))) * up).astype(bf16)
        part = _dot(hdn, wd_ref[cs, :])
        down = part if down is None else down + part
    o_ref[...] = _layer_norm(DEEPNORM_ALPHA * x2 + down, g3_ref[...], b3_ref[...])
    c_ref[...] = pointwise()


def _post(xn, att, glu, kc, vc, cw, cb, ng, nb, wpw, bpw,
          wo, g1, b1, wq, wom, g2, b2, wg, wu, wd, g3, b3, tiles_per_batch):
    t, d = xn.shape
    nt = t // POST_TM
    cw_ = CONV_WIDTH
    halo_blocks = POST_TM // CONV_HALO
    row = lambda i: (i, 0)
    const = lambda i: (0, 0)
    nxt = lambda i: (jnp.minimum(i + 1, nt - 1), 0)
    nxt_halo = lambda i: (jnp.minimum(i + 1, nt - 1) * halo_blocks - 1, 0)

    def resident(shape):
        return pl.BlockSpec(shape, const, pipeline_mode=pl.Buffered(1))

    return pl.pallas_call(
        functools.partial(_post_kernel, tiles_per_batch=tiles_per_batch),
        grid=(nt,),
        in_specs=[
            pl.BlockSpec((POST_TM, d), row),
            pl.BlockSpec((POST_TM, ATT_WIDTH), row),
            resident((POST_TM, cw_)),
            pl.BlockSpec((POST_TM, cw_), nxt),
            pl.BlockSpec((CONV_HALO, cw_), nxt_halo),
            pl.BlockSpec((1, MEM_LEN, d), lambda i: (i // tiles_per_batch, 0, 0)),
            pl.BlockSpec((1, MEM_LEN, d), lambda i: (i // tiles_per_batch, 0, 0)),
            resident((CONV_KERNEL, cw_)), resident((1, cw_)), resident((1, cw_)), resident((1, cw_)),
            resident((cw_, cw_)), resident((1, cw_)),
            resident((d, d)), resident((1, d)), resident((1, d)),
            resident((d, d)), resident((d, d)), resident((1, d)), resident((1, d)),
            resident((d, D_FF)), resident((d, D_FF)), resident((D_FF, d)),
            resident((1, d)), resident((1, d)),
        ],
        out_specs=pl.BlockSpec((POST_TM, d), row),
        out_shape=jax.ShapeDtypeStruct((t, d), jnp.float32),
        scratch_shapes=[
            pltpu.VMEM((POST_TM, d), jnp.bfloat16),
            pltpu.VMEM((POST_TM, cw_), jnp.bfloat16),
            pltpu.VMEM((CONV_GROUPS, CONV_HALO + POST_TM, CONV_GROUP_DIM), jnp.float32),
            pltpu.VMEM((POST_TM, cw_), jnp.bfloat16),
        ],
        compiler_params=pltpu.CompilerParams(
            dimension_semantics=("arbitrary",), vmem_limit_bytes=VMEM_LIMIT),
        name="post",
    )(xn, att, glu, glu, glu, kc, vc, cw, cb, ng, nb, wpw, bpw,
      wo, g1, b1, wq, wom, g2, b2, wg, wu, wd, g3, b3)


def kernel(x, mem, in_norm_g, in_norm_b, w_in, lambda_q1, lambda_k1, lambda_q2, lambda_k2,
           subln_g, conv_w, conv_b, conv_norm_g, conv_norm_b, w_pw, b_pw, w_o, ln1_g, ln1_b,
           w_q_mem, w_kv_mem, w_o_mem, ln2_g, ln2_b, w_gate, w_up, w_down, ln3_g, ln3_b):
    assert DEPTH == 1 and w_in.shape[0] == 1
    bsz, s, d = x.shape
    bf16 = jnp.bfloat16
    row = lambda v: v.reshape(1, -1)
    l = 0
    lam_init = 0.8 - 0.6 * math.exp(-0.3 * l)
    slopes2 = jnp.asarray([_alibi_slope(h) * LOG2E for h in range(ATT_HEADS)], jnp.float32)
    w = w_in[l].astype(bf16)
    wq = w[:, :QK_TOTAL]
    wk = w[:, QK_TOTAL:2 * QK_TOTAL]
    wv = w[:, 2 * QK_TOTAL:2 * QK_TOTAL + ATT_WIDTH]
    wc = w[:, 2 * QK_TOTAL + ATT_WIDTH:]
    xn, qT, k, vT, glu = _in_proj(x, row(in_norm_g), row(in_norm_b), wq, wk, wv, wc)
    att = _diff_attention(slopes2, qT, k, vT, row(lambda_q1[l]), row(lambda_k1[l]),
                          row(lambda_q2[l]), row(lambda_k2[l]),
                          subln_g[l].reshape(-1, 1), lam_init)
    kc, vc = _mem_kv(mem, w_kv_mem[l].astype(bf16))
    out = _post(xn.reshape(bsz * s, d), att.reshape(bsz * s, ATT_WIDTH),
                glu.reshape(bsz * s, CONV_WIDTH), kc, vc,
                conv_w[l], row(conv_b[l]), row(conv_norm_g[l]), row(conv_norm_b[l]),
                w_pw[l].astype(bf16), row(b_pw[l]),
                w_o[l].astype(bf16), row(ln1_g[l]), row(ln1_b[l]),
                w_q_mem[l].astype(bf16), w_o_mem[l].astype(bf16), row(ln2_g[l]), row(ln2_b[l]),
                w_gate[l].astype(bf16), w_up[l].astype(bf16), w_down[l].astype(bf16),
                row(ln3_g[l]), row(ln3_b[l]), s // POST_TM)
    return out.reshape(bsz, s, d)
```

```python
import functools
import math

import jax
import jax.numpy as jnp
from jax import lax
from jax.experimental import pallas as pl
from jax.experimental.pallas import tpu as pltpu

D_MODEL = 1024
DEPTH = 1
MEM_LEN = 256
ATT_WIDTH = 512
CONV_WIDTH = 512
ATT_HEADS = 4
ATT_HEAD_DIM = 128
QK_DIM = 64
QK_TOTAL = 512
CONV_GROUPS = 4
CONV_GROUP_DIM = 128
CONV_KERNEL = 31
MEM_HEADS = 4
MEM_HEAD_DIM = 256
D_FF = 2816
LN_EPS = 1e-5
DEEPNORM_ALPHA = (2 * DEPTH) ** 0.25

LOG2E = 1.4426950408889634
NEG_BIG = -1e30

ATT_BQ = 512
ATT_BK = 512
ATT_SUM_ROWS = 16
PROJ_TM = 512
CONV_HALO = 32
CONV_ROWS = 64
CONV_CHUNKS = 2
POST_TM = 512

VMEM_LIMIT = 60 * 1024 * 1024

_NT = (((1,), (1,)), ((), ()))


def _layer_norm(x, g, b):
    mu = jnp.mean(x, axis=-1, keepdims=True)
    xc = x - mu
    var = jnp.mean(xc * xc, axis=-1, keepdims=True)
    return xc * lax.rsqrt(var + LN_EPS) * g + b


def _alibi_slope(h):
    return 2.0 ** (-8.0 * (h + 1) / ATT_HEADS)


def _dot(a, b):
    return jnp.dot(a, b, preferred_element_type=jnp.float32)


def _dot_nt(a, b):
    return lax.dot_general(a, b, _NT, preferred_element_type=jnp.float32)


def _inproj_kernel(x_ref, g_ref, b_ref, wq_ref, wk_ref, wv_ref, wc_ref,
                   xn_ref, qT_ref, k_ref, vT_ref, glu_ref):
    tm = PROJ_TM
    xn = _layer_norm(x_ref[0], g_ref[...], b_ref[...])
    xn_ref[0] = xn
    xb = xn.astype(jnp.bfloat16)
    k_ref[0] = _dot(xb, wk_ref[...]).astype(jnp.bfloat16)
    qscale = (QK_DIM ** -0.5) * LOG2E
    qT = (_dot(xb, wq_ref[...]) * qscale).T
    vT = _dot(xb, wv_ref[...]).T
    for j in range(tm // ATT_BQ):
        qT_ref[0, j] = qT[:, j * ATT_BQ:(j + 1) * ATT_BQ].astype(jnp.bfloat16)
    for j in range(tm // ATT_BK):
        vT_ref[0, j] = vT[:, j * ATT_BK:(j + 1) * ATT_BK].astype(jnp.bfloat16)
    c_val = _dot(xb, wc_ref[:, :CONV_WIDTH])
    c_gate = _dot(xb, wc_ref[:, CONV_WIDTH:])
    glu_ref[0] = c_val * (1.0 / (1.0 + jnp.exp(-c_gate)))


def _in_proj(x, g, b, wq, wk, wv, wc):
    bsz, s, d = x.shape
    nt = s // PROJ_TM
    nbq = PROJ_TM // ATT_BQ
    nbk = PROJ_TM // ATT_BK
    const = lambda bi, si: (0, 0)
    return pl.pallas_call(
        _inproj_kernel,
        grid=(bsz, nt),
        in_specs=[
            pl.BlockSpec((1, PROJ_TM, d), lambda bi, si: (bi, si, 0)),
            pl.BlockSpec((1, d), const),
            pl.BlockSpec((1, d), const),
            pl.BlockSpec((d, QK_TOTAL), const),
            pl.BlockSpec((d, QK_TOTAL), const),
            pl.BlockSpec((d, ATT_WIDTH), const),
            pl.BlockSpec((d, 2 * CONV_WIDTH), const),
        ],
        out_specs=[
            pl.BlockSpec((1, PROJ_TM, d), lambda bi, si: (bi, si, 0)),
            pl.BlockSpec((1, nbq, QK_TOTAL, ATT_BQ), lambda bi, si: (bi, si, 0, 0)),
            pl.BlockSpec((1, PROJ_TM, QK_TOTAL), lambda bi, si: (bi, si, 0)),
            pl.BlockSpec((1, nbk, ATT_WIDTH, ATT_BK), lambda bi, si: (bi, si, 0, 0)),
            pl.BlockSpec((1, PROJ_TM, CONV_WIDTH), lambda bi, si: (bi, si, 0)),
        ],
        out_shape=[
            jax.ShapeDtypeStruct((bsz, s, d), jnp.float32),
            jax.ShapeDtypeStruct((bsz, s // ATT_BQ, QK_TOTAL, ATT_BQ), jnp.bfloat16),
            jax.ShapeDtypeStruct((bsz, s, QK_TOTAL), jnp.bfloat16),
            jax.ShapeDtypeStruct((bsz, s // ATT_BK, ATT_WIDTH, ATT_BK), jnp.bfloat16),
            jax.ShapeDtypeStruct((bsz, s, CONV_WIDTH), jnp.float32),
        ],
        compiler_params=pltpu.CompilerParams(
            dimension_semantics=("arbitrary", "arbitrary"),
            vmem_limit_bytes=VMEM_LIMIT),
        name="in_proj",
    )(x, g, b, wq, wk, wv, wc)


def _attn_kernel(slope_ref, qT_ref, k_ref, vT_ref, lq1_ref, lk1_ref, lq2_ref, lk2_ref,
                 gain_ref, o_ref, qs_ref, qsn_ref, kbias_ref, m_ref, acc_ref,
                 sa_ref, sb_ref, sc_ref, mxa_ref, mxb_ref, mxc_ref, *, lam_init):
    blk = ATT_BQ
    nq = qT_ref.shape[1]
    slope2 = slope_ref[pl.program_id(1)]

    pos = lax.broadcasted_iota(jnp.int32, (blk, ATT_HEAD_DIM), 0)
    lane = lax.broadcasted_iota(jnp.int32, (blk, ATT_HEAD_DIM), 1)
    u = slope2 * pos.astype(jnp.float32)
    hi = u.astype(jnp.bfloat16).astype(jnp.float32)
    mid = (u - hi).astype(jnp.bfloat16).astype(jnp.float32)
    lo = u - hi - mid
    cols = jnp.where(lane == 0, hi, jnp.where(lane == 1, mid, jnp.where(lane == 2, lo, 0.0)))
    kbias_ref[...] = cols.astype(jnp.bfloat16)

    row2 = lax.broadcasted_iota(jnp.int32, (ATT_HEAD_DIM, 2 * blk), 0)
    ones3 = jnp.where(row2 < 3, 1.0, 0.0).astype(jnp.bfloat16)
    qs_ref[ATT_HEAD_DIM:, :] = ones3
    qsn_ref[ATT_HEAD_DIM:, :] = ones3

    def stack_q(dst_ref, qidx):
        qT = qT_ref[0, qidx]
        row = lax.broadcasted_iota(jnp.int32, (ATT_HEAD_DIM, blk), 0)
        zero = jnp.zeros_like(qT)
        dst_ref[:ATT_HEAD_DIM, :blk] = jnp.where(row < QK_DIM, qT, zero)
        dst_ref[:ATT_HEAD_DIM, blk:] = jnp.where(row >= QK_DIM, qT, zero)

    ones_rows = jnp.ones((ATT_SUM_ROWS, blk), jnp.bfloat16)

    def produce(q_ref, qi, j, buf, masked):
        s_ref, mx_ref = buf
        start = j * blk if isinstance(j, int) else pl.multiple_of(j * blk, blk)
        kb = jnp.concatenate([k_ref[0, pl.ds(start, blk), :], kbias_ref[...]], axis=1)
        s = _dot(kb, q_ref[...])
        if masked:
            kr = lax.broadcasted_iota(jnp.int32, (blk, blk), 0)
            qc = lax.broadcasted_iota(jnp.int32, (blk, blk), 1)
            valid = (kr - qc) <= (qi - j) * blk
            s = jnp.where(jnp.concatenate([valid, valid], axis=1), s, NEG_BIG)
        s_ref[...] = s
        mx_ref[...] = jnp.max(s, axis=0, keepdims=True)

    def consume(qi, j, buf):
        s_ref, mx_ref = buf
        off = slope2 * ((j - qi) * blk).astype(jnp.float32)
        m_old = m_ref[...]
        m_new = jnp.maximum(m_old, mx_ref[...] + off)
        p = jnp.exp2(s_ref[...] - (m_new - off)).astype(jnp.bfloat16)
        alpha = jnp.exp2(m_old - m_new)
        lhs = jnp.concatenate([vT_ref[0, j], ones_rows], axis=0)
        acc_ref[...] = alpha * acc_ref[...] + _dot(lhs, p)
        m_ref[...] = m_new

    lam = (jnp.exp(jnp.sum(lq1_ref[...] * lk1_ref[...], axis=-1, keepdims=True))
           - jnp.exp(jnp.sum(lq2_ref[...] * lk2_ref[...], axis=-1, keepdims=True))
           + lam_init)

    buf_a, buf_b, buf_c = (sa_ref, mxa_ref), (sb_ref, mxb_ref), (sc_ref, mxc_ref)

    stack_q(qsn_ref, 0)
    produce(qsn_ref, 0, 0, buf_c, True)

    def qblock(qi, carry):
        qnext = jnp.minimum(qi + 1, nq - 1)
        stack_q(qs_ref, qi)
        stack_q(qsn_ref, qnext)
        m_ref[...] = jnp.full(m_ref.shape, NEG_BIG, jnp.float32)
        acc_ref[...] = jnp.zeros(acc_ref.shape, jnp.float32)

        def produce_next():
            produce(qsn_ref, qnext, 0, buf_c, False)

        @pl.when(qi == 0)
        def _():
            consume(qi, 0, buf_c)
            produce_next()

        @pl.when(qi == 1)
        def _():
            produce(qs_ref, qi, 1, buf_a, True)
            consume(qi, 0, buf_c)
            produce_next()
            consume(qi, 1, buf_a)

        @pl.when(qi >= 2)
        def _():
            produce(qs_ref, qi, 1, buf_a, False)
            consume(qi, 0, buf_c)

        npair = jnp.maximum(qi - 2, 0) // 2

        def pair(i, c):
            j = 2 * i + 1
            produce(qs_ref, qi, j + 1, buf_b, False)
            consume(qi, j, buf_a)
            produce(qs_ref, qi, j + 2, buf_a, False)
            consume(qi, j + 1, buf_b)
            return c

        def two_pairs(i, c):
            return pair(2 * i + 1, pair(2 * i, c))

        lax.fori_loop(0, npair // 2, two_pairs, 0)
        lax.fori_loop(2 * (npair // 2), npair, pair, 0)
        j0 = 2 * npair + 1
        rest = qi - j0

        @pl.when(jnp.logical_and(qi >= 2, rest == 1))
        def _():
            produce(qs_ref, qi, qi, buf_b, True)
            consume(qi, j0, buf_a)
            produce_next()
            consume(qi, qi, buf_b)

        @pl.when(jnp.logical_and(qi >= 2, rest == 2))
        def _():
            produce(qs_ref, qi, j0 + 1, buf_b, False)
            consume(qi, j0, buf_a)
            produce(qs_ref, qi, qi, buf_a, True)
            consume(qi, j0 + 1, buf_b)
            produce_next()
            consume(qi, qi, buf_a)

        acc = acc_ref[:ATT_HEAD_DIM, :]
        l = acc_ref[ATT_HEAD_DIM:ATT_HEAD_DIM + 1, :]
        oT = acc[:, :blk] / l[:, :blk] - lam * (acc[:, blk:] / l[:, blk:])
        ms = jnp.mean(oT * oT, axis=0, keepdims=True)
        oT = oT * lax.rsqrt(ms + LN_EPS) * gain_ref[...] * (1.0 - lam_init)
        o_ref[0, pl.ds(pl.multiple_of(qi * blk, blk), blk), :] = oT.T.astype(o_ref.dtype)
        return carry

    lax.fori_loop(0, nq, qblock, 0)


def _diff_attention(slopes2, qT, k, vT, lq1, lk1, lq2, lk2, gain_col, lam_init):
    bsz, nq, _, bq = qT.shape
    _, nk, _, bk = vT.shape
    assert bq == bk and nq == nk
    s = nq * bq
    vec = lambda bi, h: (0, 0)
    head = lambda bi, h: (bi, 0, h, 0)
    return pl.pallas_call(
        functools.partial(_attn_kernel, lam_init=lam_init),
        grid=(bsz, ATT_HEADS),
        in_specs=[
            pl.BlockSpec(memory_space=pltpu.SMEM),
            pl.BlockSpec((1, nq, ATT_HEAD_DIM, bq), head),
            pl.BlockSpec((1, s, ATT_HEAD_DIM), lambda bi, h: (bi, 0, h)),
            pl.BlockSpec((1, nk, ATT_HEAD_DIM, bk), head),
            pl.BlockSpec((1, QK_DIM), vec),
            pl.BlockSpec((1, QK_DIM), vec),
            pl.BlockSpec((1, QK_DIM), vec),
            pl.BlockSpec((1, QK_DIM), vec),
            pl.BlockSpec((ATT_HEAD_DIM, 1), vec),
        ],
        out_specs=pl.BlockSpec((1, s, ATT_HEAD_DIM), lambda bi, h: (bi, 0, h)),
        out_shape=jax.ShapeDtypeStruct((bsz, s, ATT_WIDTH), jnp.bfloat16),
        scratch_shapes=[
            pltpu.VMEM((2 * ATT_HEAD_DIM, 2 * bq), jnp.bfloat16),
            pltpu.VMEM((2 * ATT_HEAD_DIM, 2 * bq), jnp.bfloat16),
            pltpu.VMEM((bk, ATT_HEAD_DIM), jnp.bfloat16),
            pltpu.VMEM((1, 2 * bq), jnp.float32),
            pltpu.VMEM((ATT_HEAD_DIM + ATT_SUM_ROWS, 2 * bq), jnp.float32),
            pltpu.VMEM((bk, 2 * bq), jnp.float32),
            pltpu.VMEM((bk, 2 * bq), jnp.float32),
            pltpu.VMEM((bk, 2 * bq), jnp.float32),
            pltpu.VMEM((1, 2 * bq), jnp.float32),
            pltpu.VMEM((1, 2 * bq), jnp.float32),
            pltpu.VMEM((1, 2 * bq), jnp.float32),
        ],
        compiler_params=pltpu.CompilerParams(
            dimension_semantics=("arbitrary", "arbitrary"),
            vmem_limit_bytes=VMEM_LIMIT),
        name="diff_attn",
    )(slopes2, qT, k, vT, lq1, lk1, lq2, lk2, gain_col)


def _memkv_kernel(mem_ref, w_ref, kc_ref, vc_ref):
    mb = mem_ref[0].astype(jnp.bfloat16)
    kc_ref[0] = _dot(mb, w_ref[:, :D_MODEL]).astype(jnp.bfloat16)
    vc_ref[0] = _dot(mb, w_ref[:, D_MODEL:]).astype(jnp.bfloat16)


def _mem_kv(mem, w_kv):
    bsz, m, d = mem.shape
    return pl.pallas_call(
        _memkv_kernel,
        grid=(bsz,),
        in_specs=[
            pl.BlockSpec((1, m, d), lambda bi: (bi, 0, 0)),
            pl.BlockSpec((d, 2 * d), lambda bi: (0, 0)),
        ],
        out_specs=[
            pl.BlockSpec((1, m, d), lambda bi: (bi, 0, 0)),
            pl.BlockSpec((1, m, d), lambda bi: (bi, 0, 0)),
        ],
        out_shape=[
            jax.ShapeDtypeStruct((bsz, m, d), jnp.bfloat16),
            jax.ShapeDtypeStruct((bsz, m, d), jnp.bfloat16),
        ],
        compiler_params=pltpu.CompilerParams(
            dimension_semantics=("arbitrary",), vmem_limit_bytes=VMEM_LIMIT),
        name="mem_kv",
    )(mem, w_kv)


def _conv_taps(glu_ref, halo, cw_ref, cb_ref, ng_ref, nb_ref, xx_ref, act_ref):
    tm = POST_TM
    first = CONV_HALO - (CONV_KERNEL - 1)
    for g in range(CONV_GROUPS):
        gs = slice(g * CONV_GROUP_DIM, (g + 1) * CONV_GROUP_DIM)
        xx_ref[g, :CONV_HALO, :] = halo[:, gs]
        xx_ref[g, CONV_HALO:, :] = glu_ref[:, gs]
    for g in range(CONV_GROUPS):
        gs = slice(g * CONV_GROUP_DIM, (g + 1) * CONV_GROUP_DIM)
        for r0 in range(0, tm, CONV_ROWS * CONV_CHUNKS):
            accs = [jnp.broadcast_to(cb_ref[:, gs], (CONV_ROWS, CONV_GROUP_DIM))] * CONV_CHUNKS
            for j in range(CONV_KERNEL):
                wj = jnp.broadcast_to(cw_ref[j:j + 1, gs], (CONV_ROWS, CONV_GROUP_DIM))
                for rc in range(CONV_CHUNKS):
                    r = r0 + rc * CONV_ROWS + first + j
                    accs[rc] = accs[rc] + wj * xx_ref[g, r:r + CONV_ROWS, :]
            for rc in range(CONV_CHUNKS):
                y = _layer_norm(accs[rc], ng_ref[:, gs], nb_ref[:, gs])
                y = y * (1.0 / (1.0 + jnp.exp(-y)))
                r = r0 + rc * CONV_ROWS
                act_ref[r:r + CONV_ROWS, gs] = y.astype(jnp.bfloat16)


def _post_kernel(xn_ref, att_ref, glu0_ref, glun_ref, halo_ref, kc_ref, vc_ref,
                 cw_ref, cb_ref, ng_ref, nb_ref, wpw_ref, bpw_ref,
                 wo_ref, g1_ref, b1_ref, wq_ref, wom_ref, g2_ref, b2_ref,
                 wg_ref, wu_ref, wd_ref, g3_ref, b3_ref, o_ref,
                 ca_ref, c_ref, xx_ref, act_ref, *, tiles_per_batch):
    bf16 = jnp.bfloat16
    i = pl.program_id(0)

    def pointwise():
        return (_dot(act_ref[...], wpw_ref[...]) + bpw_ref[...]).astype(bf16)

    @pl.when(i == 0)
    def _():
        _conv_taps(glu0_ref, jnp.zeros((CONV_HALO, CONV_WIDTH), jnp.float32),
                   cw_ref, cb_ref, ng_ref, nb_ref, xx_ref, act_ref)
        c_ref[...] = pointwise()

    mix = _dot(att_ref[...], wo_ref[:ATT_WIDTH, :]) + _dot(c_ref[...], wo_ref[ATT_WIDTH:, :])
    x1 = _layer_norm(DEEPNORM_ALPHA * xn_ref[...] + mix, g1_ref[...], b1_ref[...])

    halo = halo_ref[...]
    halo = jnp.where((i + 1) % tiles_per_batch == 0, jnp.zeros_like(halo), halo)
    _conv_taps(glun_ref, halo, cw_ref, cb_ref, ng_ref, nb_ref, xx_ref, act_ref)

    qscale = (MEM_HEAD_DIM ** -0.5) * LOG2E
    qm = (_dot(x1.astype(bf16), wq_ref[...]) * qscale).astype(bf16)
    for h in range(MEM_HEADS):
        hs = slice(h * MEM_HEAD_DIM, (h + 1) * MEM_HEAD_DIM)
        logits = _dot_nt(qm[:, hs], kc_ref[0, :, hs])
        p = jnp.exp2(logits - jnp.max(logits, axis=-1, keepdims=True))
        denom = jnp.sum(p, axis=-1, keepdims=True)
        ca = _dot(p.astype(bf16), vc_ref[0, :, hs]) / denom
        ca_ref[:, hs] = ca.astype(bf16)
    x2 = _layer_norm(DEEPNORM_ALPHA * x1 + _dot(ca_ref[...], wom_ref[...]),
                     g2_ref[...], b2_ref[...])

    x2b = x2.astype(bf16)
    gate = _dot(x2b, wg_ref[...])
    up = _dot(x2b, wu_ref[...])
    hdn = (gate * (1.0 / (1.0 + jnp.exp(-gate))) * up).astype(bf16)
    o_ref[...] = _layer_norm(DEEPNORM_ALPHA * x2 + _dot(hdn, wd_ref[...]), g3_ref[...], b3_ref[...])
    c_ref[...] = pointwise()


def _post(xn, att, glu, kc, vc, cw, cb, ng, nb, wpw, bpw,
          wo, g1, b1, wq, wom, g2, b2, wg, wu, wd, g3, b3, tiles_per_batch):
    t, d = xn.shape
    nt = t // POST_TM
    cw_ = CONV_WIDTH
    halo_blocks = POST_TM // CONV_HALO
    row = lambda i: (i, 0)
    const = lambda i: (0, 0)
    nxt = lambda i: (jnp.minimum(i + 1, nt - 1), 0)
    nxt_halo = lambda i: (jnp.minimum(i + 1, nt - 1) * halo_blocks - 1, 0)

    def resident(shape):
        return pl.BlockSpec(shape, const, pipeline_mode=pl.Buffered(1))

    return pl.pallas_call(
        functools.partial(_post_kernel, tiles_per_batch=tiles_per_batch),
        grid=(nt,),
        in_specs=[
            pl.BlockSpec((POST_TM, d), row),
            pl.BlockSpec((POST_TM, ATT_WIDTH), row),
            resident((POST_TM, cw_)),
            pl.BlockSpec((POST_TM, cw_), nxt),
            pl.BlockSpec((CONV_HALO, cw_), nxt_halo),
            pl.BlockSpec((1, MEM_LEN, d), lambda i: (i // tiles_per_batch, 0, 0)),
            pl.BlockSpec((1, MEM_LEN, d), lambda i: (i // tiles_per_batch, 0, 0)),
            resident((CONV_KERNEL, cw_)), resident((1, cw_)), resident((1, cw_)), resident((1, cw_)),
            resident((cw_, cw_)), resident((1, cw_)),
            resident((d, d)), resident((1, d)), resident((1, d)),
            resident((d, d)), resident((d, d)), resident((1, d)), resident((1, d)),
            resident((d, D_FF)), resident((d, D_FF)), resident((D_FF, d)),
            resident((1, d)), resident((1, d)),
        ],
        out_specs=pl.BlockSpec((POST_TM, d), row),
        out_shape=jax.ShapeDtypeStruct((t, d), jnp.float32),
        scratch_shapes=[
            pltpu.VMEM((POST_TM, d), jnp.bfloat16),
            pltpu.VMEM((POST_TM, cw_), jnp.bfloat16),
            pltpu.VMEM((CONV_GROUPS, CONV_HALO + POST_TM, CONV_GROUP_DIM), jnp.float32),
            pltpu.VMEM((POST_TM, cw_), jnp.bfloat16),
        ],
        compiler_params=pltpu.CompilerParams(
            dimension_semantics=("arbitrary",), vmem_limit_bytes=VMEM_LIMIT),
        name="post",
    )(xn, att, glu, glu, glu, kc, vc, cw, cb, ng, nb, wpw, bpw,
      wo, g1, b1, wq, wom, g2, b2, wg, wu, wd, g3, b3)


def kernel(x, mem, in_norm_g, in_norm_b, w_in, lambda_q1, lambda_k1, lambda_q2, lambda_k2,
           subln_g, conv_w, conv_b, conv_norm_g, conv_norm_b, w_pw, b_pw, w_o, ln1_g, ln1_b,
           w_q_mem, w_kv_mem, w_o_mem, ln2_g, ln2_b, w_gate, w_up, w_down, ln3_g, ln3_b):
    assert DEPTH == 1 and w_in.shape[0] == 1
    bsz, s, d = x.shape
    bf16 = jnp.bfloat16
    row = lambda v: v.reshape(1, -1)
    l = 0
    lam_init = 0.8 - 0.6 * math.exp(-0.3 * l)
    slopes2 = jnp.asarray([_alibi_slope(h) * LOG2E for h in range(ATT_HEADS)], jnp.float32)
    w = w_in[l].astype(bf16)
    wq = w[:, :QK_TOTAL]
    wk = w[:, QK_TOTAL:2 * QK_TOTAL]
    wv = w[:, 2 * QK_TOTAL:2 * QK_TOTAL + ATT_WIDTH]
    wc = w[:, 2 * QK_TOTAL + ATT_WIDTH:]
    xn, qT, k, vT, glu = _in_proj(x, row(in_norm_g), row(in_norm_b), wq, wk, wv, wc)
    att = _diff_attention(slopes2, qT, k, vT, row(lambda_q1[l]), row(lambda_k1[l]),
                          row(lambda_q2[l]), row(lambda_k2[l]),
                          subln_g[l].reshape(-1, 1), lam_init)
    kc, vc = _mem_kv(mem, w_kv_mem[l].astype(bf16))
    out = _post(xn.reshape(bsz * s, d), att.reshape(bsz * s, ATT_WIDTH),
                glu.reshape(bsz * s, CONV_WIDTH), kc, vc,
                conv_w[l], row(conv_b[l]), row(conv_norm_g[l]), row(conv_norm_b[l]),
                w_pw[l].astype(bf16), row(b_pw[l]),
                w_o[l].astype(bf16), row(ln1_g[l]), row(ln1_b[l]),
                w_q_mem[l].astype(bf16), w_o_mem[l].astype(bf16), row(ln2_g[l]), row(ln2_b[l]),
                w_gate[l].astype(bf16), w_up[l].astype(bf16), w_down[l].astype(bf16),
                row(ln3_g[l]), row(ln3_b[l]), s // POST_TM)
    return out.reshape(bsz, s, d)
```

```python
import functools
import math

import jax
import jax.numpy as jnp
from jax import lax
from jax.experimental import pallas as pl
from jax.experimental.pallas import tpu as pltpu

D_MODEL = 1024
DEPTH = 1
MEM_LEN = 256
ATT_WIDTH = 512
CONV_WIDTH = 512
ATT_HEADS = 4
ATT_HEAD_DIM = 128
QK_DIM = 64
QK_TOTAL = 512
CONV_GROUPS = 4
CONV_GROUP_DIM = 128
CONV_KERNEL = 31
MEM_HEADS = 4
MEM_HEAD_DIM = 256
D_FF = 2816
LN_EPS = 1e-5
DEEPNORM_ALPHA = (2 * DEPTH) ** 0.25

LOG2E = 1.4426950408889634
NEG_BIG = -1e30

ATT_BQ = 512
ATT_BK = 512
ATT_SUM_ROWS = 16
PROJ_TM = 512
CONV_HALO = 32
CONV_ROWS = 64
CONV_CHUNKS = 2
POST_TM = 512

VMEM_LIMIT = 60 * 1024 * 1024

_NT = (((1,), (1,)), ((), ()))


def _layer_norm(x, g, b):
    mu = jnp.mean(x, axis=-1, keepdims=True)
    xc = x - mu
    var = jnp.mean(xc * xc, axis=-1, keepdims=True)
    return xc * lax.rsqrt(var + LN_EPS) * g + b


def _alibi_slope(h):
    return 2.0 ** (-8.0 * (h + 1) / ATT_HEADS)


def _dot(a, b):
    return jnp.dot(a, b, preferred_element_type=jnp.float32)


def _dot_nt(a, b):
    return lax.dot_general(a, b, _NT, preferred_element_type=jnp.float32)


def _inproj_kernel(x_ref, g_ref, b_ref, wq_ref, wk_ref, wv_ref, wc_ref,
                   xn_ref, qT_ref, k_ref, vT_ref, glu_ref):
    tm = PROJ_TM
    xn = _layer_norm(x_ref[0], g_ref[...], b_ref[...])
    xn_ref[0] = xn
    xb = xn.astype(jnp.bfloat16)
    k_ref[0] = _dot(xb, wk_ref[...]).astype(jnp.bfloat16)
    qscale = (QK_DIM ** -0.5) * LOG2E
    qT = (_dot(xb, wq_ref[...]) * qscale).T
    vT = _dot(xb, wv_ref[...]).T
    for j in range(tm // ATT_BQ):
        qT_ref[0, j] = qT[:, j * ATT_BQ:(j + 1) * ATT_BQ].astype(jnp.bfloat16)
    for j in range(tm // ATT_BK):
        vT_ref[0, j] = vT[:, j * ATT_BK:(j + 1) * ATT_BK].astype(jnp.bfloat16)
    c_val = _dot(xb, wc_ref[:, :CONV_WIDTH])
    c_gate = _dot(xb, wc_ref[:, CONV_WIDTH:])
    glu_ref[0] = c_val * (1.0 / (1.0 + jnp.exp(-c_gate)))


def _in_proj(x, g, b, wq, wk, wv, wc):
    bsz, s, d = x.shape
    nt = s // PROJ_TM
    nbq = PROJ_TM // ATT_BQ
    nbk = PROJ_TM // ATT_BK
    const = lambda bi, si: (0, 0)
    return pl.pallas_call(
        _inproj_kernel,
        grid=(bsz, nt),
        in_specs=[
            pl.BlockSpec((1, PROJ_TM, d), lambda bi, si: (bi, si, 0)),
            pl.BlockSpec((1, d), const),
            pl.BlockSpec((1, d), const),
            pl.BlockSpec((d, QK_TOTAL), const),
            pl.BlockSpec((d, QK_TOTAL), const),
            pl.BlockSpec((d, ATT_WIDTH), const),
            pl.BlockSpec((d, 2 * CONV_WIDTH), const),
        ],
        out_specs=[
            pl.BlockSpec((1, PROJ_TM, d), lambda bi, si: (bi, si, 0)),
            pl.BlockSpec((1, nbq, QK_TOTAL, ATT_BQ), lambda bi, si: (bi, si, 0, 0)),
            pl.BlockSpec((1, PROJ_TM, QK_TOTAL), lambda bi, si: (bi, si, 0)),
            pl.BlockSpec((1, nbk, ATT_WIDTH, ATT_BK), lambda bi, si: (bi, si, 0, 0)),
            pl.BlockSpec((1, PROJ_TM, CONV_WIDTH), lambda bi, si: (bi, si, 0)),
        ],
        out_shape=[
            jax.ShapeDtypeStruct((bsz, s, d), jnp.float32),
            jax.ShapeDtypeStruct((bsz, s // ATT_BQ, QK_TOTAL, ATT_BQ), jnp.bfloat16),
            jax.ShapeDtypeStruct((bsz, s, QK_TOTAL), jnp.bfloat16),
            jax.ShapeDtypeStruct((bsz, s // ATT_BK, ATT_WIDTH, ATT_BK), jnp.bfloat16),
            jax.ShapeDtypeStruct((bsz, s, CONV_WIDTH), jnp.float32),
        ],
        compiler_params=pltpu.CompilerParams(
            dimension_semantics=("arbitrary", "arbitrary"),
            vmem_limit_bytes=VMEM_LIMIT),
        name="in_proj",
    )(x, g, b, wq, wk, wv, wc)


def _attn_kernel(slope_ref, qT_ref, k_ref, vT_ref, lq1_ref, lk1_ref, lq2_ref, lk2_ref,
                 gain_ref, o_ref, qs_ref, qsn_ref, kbias_ref, m_ref, acc_ref,
                 sa_ref, sb_ref, sc_ref, mxa_ref, mxb_ref, mxc_ref, *, lam_init):
    blk = ATT_BQ
    nq = qT_ref.shape[1]
    slope2 = slope_ref[pl.program_id(1)]

    pos = lax.broadcasted_iota(jnp.int32, (blk, ATT_HEAD_DIM), 0)
    lane = lax.broadcasted_iota(jnp.int32, (blk, ATT_HEAD_DIM), 1)
    u = slope2 * pos.astype(jnp.float32)
    hi = u.astype(jnp.bfloat16).astype(jnp.float32)
    mid = (u - hi).astype(jnp.bfloat16).astype(jnp.float32)
    lo = u - hi - mid
    cols = jnp.where(lane == 0, hi, jnp.where(lane == 1, mid, jnp.where(lane == 2, lo, 0.0)))
    kbias_ref[...] = cols.astype(jnp.bfloat16)

    row2 = lax.broadcasted_iota(jnp.int32, (ATT_HEAD_DIM, 2 * blk), 0)
    ones3 = jnp.where(row2 < 3, 1.0, 0.0).astype(jnp.bfloat16)
    qs_ref[ATT_HEAD_DIM:, :] = ones3
    qsn_ref[ATT_HEAD_DIM:, :] = ones3

    def stack_q(dst_ref, qidx):
        qT = qT_ref[0, qidx]
        row = lax.broadcasted_iota(jnp.int32, (ATT_HEAD_DIM, blk), 0)
        zero = jnp.zeros_like(qT)
        dst_ref[:ATT_HEAD_DIM, :blk] = jnp.where(row < QK_DIM, qT, zero)
        dst_ref[:ATT_HEAD_DIM, blk:] = jnp.where(row >= QK_DIM, qT, zero)

    ones_rows = jnp.ones((ATT_SUM_ROWS, blk), jnp.bfloat16)

    def produce(q_ref, qi, j, buf, masked):
        s_ref, mx_ref = buf
        start = j * blk if isinstance(j, int) else pl.multiple_of(j * blk, blk)
        kb = jnp.concatenate([k_ref[0, pl.ds(start, blk), :], kbias_ref[...]], axis=1)
        s = _dot(kb, q_ref[...])
        if masked:
            kr = lax.broadcasted_iota(jnp.int32, (blk, blk), 0)
            qc = lax.broadcasted_iota(jnp.int32, (blk, blk), 1)
            valid = (kr - qc) <= (qi - j) * blk
            s = jnp.where(jnp.concatenate([valid, valid], axis=1), s, NEG_BIG)
        s_ref[...] = s
        mx_ref[...] = jnp.max(s, axis=0, keepdims=True)

    def consume(qi, j, buf):
        s_ref, mx_ref = buf
        off = slope2 * jnp.asarray((j - qi) * blk, jnp.float32)
        m_old = m_ref[...]
        m_new = jnp.maximum(m_old, mx_ref[...] + off)
        p = jnp.exp2(s_ref[...] - (m_new - off)).astype(jnp.bfloat16)
        alpha = jnp.exp2(m_old - m_new)
        lhs = jnp.concatenate([vT_ref[0, j], ones_rows], axis=0)
        acc_ref[...] = alpha * acc_ref[...] + _dot(lhs, p)
        m_ref[...] = m_new

    lam = (jnp.exp(jnp.sum(lq1_ref[...] * lk1_ref[...], axis=-1, keepdims=True))
           - jnp.exp(jnp.sum(lq2_ref[...] * lk2_ref[...], axis=-1, keepdims=True))
           + lam_init)

    buf_a, buf_b, buf_c = (sa_ref, mxa_ref), (sb_ref, mxb_ref), (sc_ref, mxc_ref)

    stack_q(qsn_ref, 0)
    produce(qsn_ref, 0, 0, buf_c, True)

    def qblock(qi, carry):
        qnext = jnp.minimum(qi + 1, nq - 1)
        stack_q(qs_ref, qi)
        stack_q(qsn_ref, qnext)
        m_ref[...] = jnp.full(m_ref.shape, NEG_BIG, jnp.float32)
        acc_ref[...] = jnp.zeros(acc_ref.shape, jnp.float32)

        def produce_next():
            produce(qsn_ref, qnext, 0, buf_c, False)

        @pl.when(qi == 0)
        def _():
            consume(qi, 0, buf_c)
            produce_next()

        @pl.when(qi == 1)
        def _():
            produce(qs_ref, qi, 1, buf_a, True)
            consume(qi, 0, buf_c)
            produce_next()
            consume(qi, 1, buf_a)

        @pl.when(qi >= 2)
        def _():
            produce(qs_ref, qi, 1, buf_a, False)
            consume(qi, 0, buf_c)

        npair = jnp.maximum(qi - 2, 0) // 2

        def pair(i, c):
            j = 2 * i + 1
            produce(qs_ref, qi, j + 1, buf_b, False)
            consume(qi, j, buf_a)
            produce(qs_ref, qi, j + 2, buf_a, False)
            consume(qi, j + 1, buf_b)
            return c

        def pairs(count):
            def body(i, c):
                for u in range(count):
                    c = pair(count * i + u, c)
                return c
            return body

        n4 = npair // 4
        n2 = (npair - 4 * n4) // 2
        lax.fori_loop(0, n4, pairs(4), 0)
        lax.fori_loop(2 * n4, 2 * n4 + n2, pairs(2), 0)
        lax.fori_loop(4 * n4 + 2 * n2, npair, pair, 0)
        j0 = 2 * npair + 1
        rest = qi - j0

        @pl.when(jnp.logical_and(qi >= 2, rest == 1))
        def _():
            produce(qs_ref, qi, qi, buf_b, True)
            consume(qi, j0, buf_a)
            produce_next()
            consume(qi, qi, buf_b)

        @pl.when(jnp.logical_and(qi >= 2, rest == 2))
        def _():
            produce(qs_ref, qi, j0 + 1, buf_b, False)
            consume(qi, j0, buf_a)
            produce(qs_ref, qi, qi, buf_a, True)
            consume(qi, j0 + 1, buf_b)
            produce_next()
            consume(qi, qi, buf_a)

        acc = acc_ref[:ATT_HEAD_DIM, :]
        l = acc_ref[ATT_HEAD_DIM:ATT_HEAD_DIM + 1, :]
        oT = acc[:, :blk] / l[:, :blk] - lam * (acc[:, blk:] / l[:, blk:])
        ms = jnp.mean(oT * oT, axis=0, keepdims=True)
        oT = oT * lax.rsqrt(ms + LN_EPS) * gain_ref[...] * (1.0 - lam_init)
        o_ref[0, pl.ds(pl.multiple_of(qi * blk, blk), blk), :] = oT.T.astype(o_ref.dtype)
        return carry

    lax.fori_loop(0, nq, qblock, 0)


def _diff_attention(slopes2, qT, k, vT, lq1, lk1, lq2, lk2, gain_col, lam_init):
    bsz, nq, _, bq = qT.shape
    _, nk, _, bk = vT.shape
    assert bq == bk and nq == nk
    s = nq * bq
    vec = lambda bi, h: (0, 0)
    head = lambda bi, h: (bi, 0, h, 0)
    return pl.pallas_call(
        functools.partial(_attn_kernel, lam_init=lam_init),
        grid=(bsz, ATT_HEADS),
        in_specs=[
            pl.BlockSpec(memory_space=pltpu.SMEM),
            pl.BlockSpec((1, nq, ATT_HEAD_DIM, bq), head),
            pl.BlockSpec((1, s, ATT_HEAD_DIM), lambda bi, h: (bi, 0, h)),
            pl.BlockSpec((1, nk, ATT_HEAD_DIM, bk), head),
            pl.BlockSpec((1, QK_DIM), vec),
            pl.BlockSpec((1, QK_DIM), vec),
            pl.BlockSpec((1, QK_DIM), vec),
            pl.BlockSpec((1, QK_DIM), vec),
            pl.BlockSpec((ATT_HEAD_DIM, 1), vec),
        ],
        out_specs=pl.BlockSpec((1, s, ATT_HEAD_DIM), lambda bi, h: (bi, 0, h)),
        out_shape=jax.ShapeDtypeStruct((bsz, s, ATT_WIDTH), jnp.bfloat16),
        scratch_shapes=[
            pltpu.VMEM((2 * ATT_HEAD_DIM, 2 * bq), jnp.bfloat16),
            pltpu.VMEM((2 * ATT_HEAD_DIM, 2 * bq), jnp.bfloat16),
            pltpu.VMEM((bk, ATT_HEAD_DIM), jnp.bfloat16),
            pltpu.VMEM((1, 2 * bq), jnp.float32),
            pltpu.VMEM((ATT_HEAD_DIM + ATT_SUM_ROWS, 2 * bq), jnp.float32),
            pltpu.VMEM((bk, 2 * bq), jnp.float32),
            pltpu.VMEM((bk, 2 * bq), jnp.float32),
            pltpu.VMEM((bk, 2 * bq), jnp.float32),
            pltpu.VMEM((1, 2 * bq), jnp.float32),
            pltpu.VMEM((1, 2 * bq), jnp.float32),
            pltpu.VMEM((1, 2 * bq), jnp.float32),
        ],
        compiler_params=pltpu.CompilerParams(
            dimension_semantics=("arbitrary", "arbitrary"),
            vmem_limit_bytes=VMEM_LIMIT),
        name="diff_attn",
    )(slopes2, qT, k, vT, lq1, lk1, lq2, lk2, gain_col)


def _memkv_kernel(mem_ref, w_ref, kc_ref, vc_ref):
    mb = mem_ref[0].astype(jnp.bfloat16)
    kc_ref[0] = _dot(mb, w_ref[:, :D_MODEL]).astype(jnp.bfloat16)
    vc_ref[0] = _dot(mb, w_ref[:, D_MODEL:]).astype(jnp.bfloat16)


def _mem_kv(mem, w_kv):
    bsz, m, d = mem.shape
    return pl.pallas_call(
        _memkv_kernel,
        grid=(bsz,),
        in_specs=[
            pl.BlockSpec((1, m, d), lambda bi: (bi, 0, 0)),
            pl.BlockSpec((d, 2 * d), lambda bi: (0, 0)),
        ],
        out_specs=[
            pl.BlockSpec((1, m, d), lambda bi: (bi, 0, 0)),
            pl.BlockSpec((1, m, d), lambda bi: (bi, 0, 0)),
        ],
        out_shape=[
            jax.ShapeDtypeStruct((bsz, m, d), jnp.bfloat16),
            jax.ShapeDtypeStruct((bsz, m, d), jnp.bfloat16),
        ],
        compiler_params=pltpu.CompilerParams(
            dimension_semantics=("arbitrary",), vmem_limit_bytes=VMEM_LIMIT),
        name="mem_kv",
    )(mem, w_kv)


def _conv_pieces(glu_ref, halo, cw_ref, cb_ref, ng_ref, nb_ref, xx_ref, act_ref):
    tm = POST_TM
    first = CONV_HALO - (CONV_KERNEL - 1)

    def fill():
        for g in range(CONV_GROUPS):
            gs = slice(g * CONV_GROUP_DIM, (g + 1) * CONV_GROUP_DIM)
            xx_ref[g, :CONV_HALO, :] = halo[:, gs]
            xx_ref[g, CONV_HALO:, :] = glu_ref[:, gs]

    def sweep(g, r0):
        gs = slice(g * CONV_GROUP_DIM, (g + 1) * CONV_GROUP_DIM)
        accs = [jnp.broadcast_to(cb_ref[:, gs], (CONV_ROWS, CONV_GROUP_DIM))] * CONV_CHUNKS
        for j in range(CONV_KERNEL):
            wj = jnp.broadcast_to(cw_ref[j:j + 1, gs], (CONV_ROWS, CONV_GROUP_DIM))
            for rc in range(CONV_CHUNKS):
                r = r0 + rc * CONV_ROWS + first + j
                accs[rc] = accs[rc] + wj * xx_ref[g, r:r + CONV_ROWS, :]
        for rc in range(CONV_CHUNKS):
            y = _layer_norm(accs[rc], ng_ref[:, gs], nb_ref[:, gs])
            y = y * (1.0 / (1.0 + jnp.exp(-y)))
            r = r0 + rc * CONV_ROWS
            act_ref[r:r + CONV_ROWS, gs] = y.astype(jnp.bfloat16)

    return [fill] + [functools.partial(sweep, g, r0) for g in range(CONV_GROUPS)
                     for r0 in range(0, tm, CONV_ROWS * CONV_CHUNKS)]


def _post_kernel(xn_ref, att_ref, glu0_ref, glun_ref, halo_ref, kc_ref, vc_ref,
                 cw_ref, cb_ref, ng_ref, nb_ref, wpw_ref, bpw_ref,
                 wo_ref, g1_ref, b1_ref, wq_ref, wom_ref, g2_ref, b2_ref,
                 wg_ref, wu_ref, wd_ref, g3_ref, b3_ref, o_ref,
                 ca_ref, c_ref, xx_ref, act_ref, *, tiles_per_batch):
    bf16 = jnp.bfloat16
    i = pl.program_id(0)

    def pointwise():
        return (_dot(act_ref[...], wpw_ref[...]) + bpw_ref[...]).astype(bf16)

    @pl.when(i == 0)
    def _():
        for piece in _conv_pieces(glu0_ref, jnp.zeros((CONV_HALO, CONV_WIDTH), jnp.float32),
                                  cw_ref, cb_ref, ng_ref, nb_ref, xx_ref, act_ref):
            piece()
        c_ref[...] = pointwise()

    mix = _dot(att_ref[...], wo_ref[:ATT_WIDTH, :]) + _dot(c_ref[...], wo_ref[ATT_WIDTH:, :])

    x1 = _layer_norm(DEEPNORM_ALPHA * xn_ref[...] + mix, g1_ref[...], b1_ref[...])

    halo = halo_ref[...]
    halo = jnp.where((i + 1) % tiles_per_batch == 0, jnp.zeros_like(halo), halo)
    for piece in _conv_pieces(glun_ref, halo, cw_ref, cb_ref, ng_ref, nb_ref, xx_ref, act_ref):
        piece()

    qscale = (MEM_HEAD_DIM ** -0.5) * LOG2E
    qm = (_dot(x1.astype(bf16), wq_ref[...]) * qscale).astype(bf16)
    for h in range(MEM_HEADS):
        hs = slice(h * MEM_HEAD_DIM, (h + 1) * MEM_HEAD_DIM)
        logits = _dot_nt(qm[:, hs], kc_ref[0, :, hs])
        p = jnp.exp2(logits - jnp.max(logits, axis=-1, keepdims=True))
        denom = jnp.sum(p, axis=-1, keepdims=True)
        ca = _dot(p.astype(bf16), vc_ref[0, :, hs]) / denom
        ca_ref[:, hs] = ca.astype(bf16)
    x2 = _layer_norm(DEEPNORM_ALPHA * x1 + _dot(ca_ref[...], wom_ref[...]),
                     g2_ref[...], b2_ref[...])

    x2b = x2.astype(bf16)
    gate = _dot(x2b, wg_ref[...])
    up = _dot(x2b, wu_ref[...])
    hdn = (gate * (1.0 / (1.0 + jnp.exp(-gate))) * up).astype(bf16)
    o_ref[...] = _layer_norm(DEEPNORM_ALPHA * x2 + _dot(hdn, wd_ref[...]), g3_ref[...], b3_ref[...])
    c_ref[...] = pointwise()


def _post(xn, att, glu, kc, vc, cw, cb, ng, nb, wpw, bpw,
          wo, g1, b1, wq, wom, g2, b2, wg, wu, wd, g3, b3, tiles_per_batch):
    t, d = xn.shape
    nt = t // POST_TM
    cw_ = CONV_WIDTH
    halo_blocks = POST_TM // CONV_HALO
    row = lambda i: (i, 0)
    const = lambda i: (0, 0)
    nxt = lambda i: (jnp.minimum(i + 1, nt - 1), 0)
    nxt_halo = lambda i: (jnp.minimum(i + 1, nt - 1) * halo_blocks - 1, 0)

    def resident(shape):
        return pl.BlockSpec(shape, const, pipeline_mode=pl.Buffered(1))

    return pl.pallas_call(
        functools.partial(_post_kernel, tiles_per_batch=tiles_per_batch),
        grid=(nt,),
        in_specs=[
            pl.BlockSpec((POST_TM, d), row),
            pl.BlockSpec((POST_TM, ATT_WIDTH), row),
            resident((POST_TM, cw_)),
            pl.BlockSpec((POST_TM, cw_), nxt),
            pl.BlockSpec((CONV_HALO, cw_), nxt_halo),
            pl.BlockSpec((1, MEM_LEN, d), lambda i: (i // tiles_per_batch, 0, 0)),
            pl.BlockSpec((1, MEM_LEN, d), lambda i: (i // tiles_per_batch, 0, 0)),
            resident((CONV_KERNEL, cw_)), resident((1, cw_)), resident((1, cw_)), resident((1, cw_)),
            resident((cw_, cw_)), resident((1, cw_)),
            resident((d, d)), resident((1, d)), resident((1, d)),
            resident((d, d)), resident((d, d)), resident((1, d)), resident((1, d)),
            resident((d, D_FF)), resident((d, D_FF)), resident((D_FF, d)),
            resident((1, d)), resident((1, d)),
        ],
        out_specs=pl.BlockSpec((POST_TM, d), row),
        out_shape=jax.ShapeDtypeStruct((t, d), jnp.float32),
        scratch_shapes=[
            pltpu.VMEM((POST_TM, d), jnp.bfloat16),
            pltpu.VMEM((POST_TM, cw_), jnp.bfloat16),
            pltpu.VMEM((CONV_GROUPS, CONV_HALO + POST_TM, CONV_GROUP_DIM), jnp.float32),
            pltpu.VMEM((POST_TM, cw_), jnp.bfloat16),
        ],
        compiler_params=pltpu.CompilerParams(
            dimension_semantics=("arbitrary",), vmem_limit_bytes=VMEM_LIMIT),
        name="post",
    )(xn, att, glu, glu, glu, kc, vc, cw, cb, ng, nb, wpw, bpw,
      wo, g1, b1, wq, wom, g2, b2, wg, wu, wd, g3, b3)


def kernel(x, mem, in_norm_g, in_norm_b, w_in, lambda_q1, lambda_k1, lambda_q2, lambda_k2,
           subln_g, conv_w, conv_b, conv_norm_g, conv_norm_b, w_pw, b_pw, w_o, ln1_g, ln1_b,
           w_q_mem, w_kv_mem, w_o_mem, ln2_g, ln2_b, w_gate, w_up, w_down, ln3_g, ln3_b):
    assert DEPTH == 1 and w_in.shape[0] == 1
    bsz, s, d = x.shape
    bf16 = jnp.bfloat16
    row = lambda v: v.reshape(1, -1)
    l = 0
    lam_init = 0.8 - 0.6 * math.exp(-0.3 * l)
    slopes2 = jnp.asarray([_alibi_slope(h) * LOG2E for h in range(ATT_HEADS)], jnp.float32)
    w = w_in[l].astype(bf16)
    wq = w[:, :QK_TOTAL]
    wk = w[:, QK_TOTAL:2 * QK_TOTAL]
    wv = w[:, 2 * QK_TOTAL:2 * QK_TOTAL + ATT_WIDTH]
    wc = w[:, 2 * QK_TOTAL + ATT_WIDTH:]
    xn, qT, k, vT, glu = _in_proj(x, row(in_norm_g), row(in_norm_b), wq, wk, wv, wc)
    att = _diff_attention(slopes2, qT, k, vT, row(lambda_q1[l]), row(lambda_k1[l]),
                          row(lambda_q2[l]), row(lambda_k2[l]),
                          subln_g[l].reshape(-1, 1), lam_init)
    kc, vc = _mem_kv(mem, w_kv_mem[l].astype(bf16))
    out = _post(xn.reshape(bsz * s, d), att.reshape(bsz * s, ATT_WIDTH),
                glu.reshape(bsz * s, CONV_WIDTH), kc, vc,
                conv_w[l], row(conv_b[l]), row(conv_norm_g[l]), row(conv_norm_b[l]),
                w_pw[l].astype(bf16), row(b_pw[l]),
                w_o[l].astype(bf16), row(ln1_g[l]), row(ln1_b[l]),
                w_q_mem[l].astype(bf16), w_o_mem[l].astype(bf16), row(ln2_g[l]), row(ln2_b[l]),
                w_gate[l].astype(bf16), w_up[l].astype(bf16), w_down[l].astype(bf16),
                row(ln3_g[l]), row(ln3_b[l]), s // POST_TM)
    return out.reshape(bsz, s, d)
```

```python
import functools
import math

import jax
import jax.numpy as jnp
from jax import lax
from jax.experimental import pallas as pl
from jax.experimental.pallas import tpu as pltpu

D_MODEL = 1024
DEPTH = 1
MEM_LEN = 256
ATT_WIDTH = 512
CONV_WIDTH = 512
ATT_HEADS = 4
ATT_HEAD_DIM = 128
QK_DIM = 64
QK_TOTAL = 512
CONV_GROUPS = 4
CONV_GROUP_DIM = 128
CONV_KERNEL = 31
MEM_HEADS = 4
MEM_HEAD_DIM = 256
D_FF = 2816
LN_EPS = 1e-5
DEEPNORM_ALPHA = (2 * DEPTH) ** 0.25

LOG2E = 1.4426950408889634
NEG_BIG = -1e30
SKIP_LOG2 = 152.0

ATT_BQ = 512
ATT_BK = 512
ATT_SUM_ROWS = 16
PROJ_TM = 512
CONV_HALO = 32
CONV_ROWS = 64
CONV_CHUNKS = 2
POST_TM = 512

VMEM_LIMIT = 60 * 1024 * 1024

_NT = (((1,), (1,)), ((), ()))


def _layer_norm(x, g, b):
    mu = jnp.mean(x, axis=-1, keepdims=True)
    xc = x - mu
    var = jnp.mean(xc * xc, axis=-1, keepdims=True)
    return xc * lax.rsqrt(var + LN_EPS) * g + b


def _alibi_slope(h):
    return 2.0 ** (-8.0 * (h + 1) / ATT_HEADS)


def _dot(a, b):
    return jnp.dot(a, b, preferred_element_type=jnp.float32)


def _dot_nt(a, b):
    return lax.dot_general(a, b, _NT, preferred_element_type=jnp.float32)


def _inproj_kernel(x_ref, g_ref, b_ref, wq_ref, wk_ref, wv_ref, wc_ref,
                   xn_ref, qT_ref, k_ref, vT_ref, glu_ref):
    tm = PROJ_TM
    xn = _layer_norm(x_ref[0], g_ref[...], b_ref[...])
    xn_ref[0] = xn
    xb = xn.astype(jnp.bfloat16)
    k_ref[0] = _dot(xb, wk_ref[...]).astype(jnp.bfloat16)
    qscale = (QK_DIM ** -0.5) * LOG2E
    qT = (_dot(xb, wq_ref[...]) * qscale).T
    vT = _dot(xb, wv_ref[...]).T
    for j in range(tm // ATT_BQ):
        qT_ref[0, j] = qT[:, j * ATT_BQ:(j + 1) * ATT_BQ].astype(jnp.bfloat16)
    for j in range(tm // ATT_BK):
        vT_ref[0, j] = vT[:, j * ATT_BK:(j + 1) * ATT_BK].astype(jnp.bfloat16)
    c_val = _dot(xb, wc_ref[:, :CONV_WIDTH])
    c_gate = _dot(xb, wc_ref[:, CONV_WIDTH:])
    glu_ref[0] = c_val * (1.0 / (1.0 + jnp.exp(-c_gate)))


def _in_proj(x, g, b, wq, wk, wv, wc):
    bsz, s, d = x.shape
    nt = s // PROJ_TM
    nbq = PROJ_TM // ATT_BQ
    nbk = PROJ_TM // ATT_BK
    const = lambda bi, si: (0, 0)
    return pl.pallas_call(
        _inproj_kernel,
        grid=(bsz, nt),
        in_specs=[
            pl.BlockSpec((1, PROJ_TM, d), lambda bi, si: (bi, si, 0)),
            pl.BlockSpec((1, d), const),
            pl.BlockSpec((1, d), const),
            pl.BlockSpec((d, QK_TOTAL), const),
            pl.BlockSpec((d, QK_TOTAL), const),
            pl.BlockSpec((d, ATT_WIDTH), const),
            pl.BlockSpec((d, 2 * CONV_WIDTH), const),
        ],
        out_specs=[
            pl.BlockSpec((1, PROJ_TM, d), lambda bi, si: (bi, si, 0)),
            pl.BlockSpec((1, nbq, QK_TOTAL, ATT_BQ), lambda bi, si: (bi, si, 0, 0)),
            pl.BlockSpec((1, PROJ_TM, QK_TOTAL), lambda bi, si: (bi, si, 0)),
            pl.BlockSpec((1, nbk, ATT_WIDTH, ATT_BK), lambda bi, si: (bi, si, 0, 0)),
            pl.BlockSpec((1, PROJ_TM, CONV_WIDTH), lambda bi, si: (bi, si, 0)),
        ],
        out_shape=[
            jax.ShapeDtypeStruct((bsz, s, d), jnp.float32),
            jax.ShapeDtypeStruct((bsz, s // ATT_BQ, QK_TOTAL, ATT_BQ), jnp.bfloat16),
            jax.ShapeDtypeStruct((bsz, s, QK_TOTAL), jnp.bfloat16),
            jax.ShapeDtypeStruct((bsz, s // ATT_BK, ATT_WIDTH, ATT_BK), jnp.bfloat16),
            jax.ShapeDtypeStruct((bsz, s, CONV_WIDTH), jnp.float32),
        ],
        compiler_params=pltpu.CompilerParams(
            dimension_semantics=("arbitrary", "arbitrary"),
            vmem_limit_bytes=VMEM_LIMIT),
        name="in_proj",
    )(x, g, b, wq, wk, wv, wc)


def _attn_kernel(slope_ref, qT_ref, k_ref, vT_ref, lq1_ref, lk1_ref, lq2_ref, lk2_ref,
                 gain_ref, o_ref, qs_ref, qsn_ref, kbias_ref, m_ref, acc_ref,
                 sa_ref, sb_ref, sc_ref, mxa_ref, mxb_ref, mxc_ref, *, lam_init):
    blk = ATT_BQ
    nq = qT_ref.shape[1]
    slope2 = slope_ref[pl.program_id(1)]

    pos = lax.broadcasted_iota(jnp.int32, (blk, ATT_HEAD_DIM), 0)
    lane = lax.broadcasted_iota(jnp.int32, (blk, ATT_HEAD_DIM), 1)
    u = slope2 * pos.astype(jnp.float32)
    hi = u.astype(jnp.bfloat16).astype(jnp.float32)
    mid = (u - hi).astype(jnp.bfloat16).astype(jnp.float32)
    lo = u - hi - mid
    cols = jnp.where(lane == 0, hi, jnp.where(lane == 1, mid, jnp.where(lane == 2, lo, 0.0)))
    kbias_ref[...] = cols.astype(jnp.bfloat16)

    row2 = lax.broadcasted_iota(jnp.int32, (ATT_HEAD_DIM, 2 * blk), 0)
    ones3 = jnp.where(row2 < 3, 1.0, 0.0).astype(jnp.bfloat16)
    qs_ref[ATT_HEAD_DIM:, :] = ones3
    qsn_ref[ATT_HEAD_DIM:, :] = ones3

    def stack_q(dst_ref, qidx):
        qT = qT_ref[0, qidx]
        row = lax.broadcasted_iota(jnp.int32, (ATT_HEAD_DIM, blk), 0)
        zero = jnp.zeros_like(qT)
        dst_ref[:ATT_HEAD_DIM, :blk] = jnp.where(row < QK_DIM, qT, zero)
        dst_ref[:ATT_HEAD_DIM, blk:] = jnp.where(row >= QK_DIM, qT, zero)

    ones_rows = jnp.ones((ATT_SUM_ROWS, blk), jnp.bfloat16)

    def produce(q_ref, qi, j, buf, masked):
        s_ref, mx_ref = buf
        start = j * blk if isinstance(j, int) else pl.multiple_of(j * blk, blk)
        kb = jnp.concatenate([k_ref[0, pl.ds(start, blk), :], kbias_ref[...]], axis=1)
        s = _dot(kb, q_ref[...])
        if masked:
            kr = lax.broadcasted_iota(jnp.int32, (blk, blk), 0)
            qc = lax.broadcasted_iota(jnp.int32, (blk, blk), 1)
            valid = (kr - qc) <= (qi - j) * blk
            s = jnp.where(jnp.concatenate([valid, valid], axis=1), s, NEG_BIG)
        s_ref[...] = s
        mx_ref[...] = jnp.max(s, axis=0, keepdims=True)

    def consume(qi, j, buf):
        s_ref, mx_ref = buf
        off = slope2 * jnp.asarray((j - qi) * blk, jnp.float32)
        m_old = m_ref[...]
        m_new = jnp.maximum(m_old, mx_ref[...] + off)
        p = jnp.exp2(s_ref[...] - (m_new - off)).astype(jnp.bfloat16)
        alpha = jnp.exp2(m_old - m_new)
        lhs = jnp.concatenate([vT_ref[0, j], ones_rows], axis=0)
        acc_ref[...] = alpha * acc_ref[...] + _dot(lhs, p)
        m_ref[...] = m_new

    lam = (jnp.exp(jnp.sum(lq1_ref[...] * lk1_ref[...], axis=-1, keepdims=True))
           - jnp.exp(jnp.sum(lq2_ref[...] * lk2_ref[...], axis=-1, keepdims=True))
           + lam_init)

    kabs = jnp.max(jnp.abs(k_ref[0].astype(jnp.float32)), axis=0, keepdims=True)
    row8 = lax.broadcasted_iota(jnp.int32, (8, ATT_HEAD_DIM), 0)
    kabs8 = jnp.where(row8 == 0, kabs, 0.0).astype(jnp.bfloat16)
    inv_chunk_bias = 1.0 / (slope2 * blk)

    buf_a, buf_b, buf_c = (sa_ref, mxa_ref), (sb_ref, mxb_ref), (sc_ref, mxc_ref)

    stack_q(qsn_ref, 0)
    produce(qsn_ref, 0, 0, buf_c, True)

    def qblock(qi, carry):
        qnext = jnp.minimum(qi + 1, nq - 1)
        stack_q(qs_ref, qi)
        stack_q(qsn_ref, qnext)
        m_ref[...] = jnp.full(m_ref.shape, NEG_BIG, jnp.float32)
        acc_ref[...] = jnp.zeros(acc_ref.shape, jnp.float32)

        def step(t, buf):
            consume(qi, qi - t, buf)

        def issue(t, buf):
            produce(qs_ref, qi, qi - t, buf, False)

        def produce_next():
            produce(qsn_ref, qnext, qnext, buf_c, True)

        @pl.when(qi == 0)
        def _():
            step(0, buf_c)
            produce_next()

        @pl.when(qi >= 1)
        def _():
            issue(1, buf_a)
            step(0, buf_c)

        score_bound = _dot(kabs8, jnp.abs(qs_ref[:ATT_HEAD_DIM, :]))[:1, :]
        gap = jnp.max(score_bound - m_ref[...], axis=1, keepdims=True) + SKIP_LOG2
        need = jnp.floor(gap * inv_chunk_bias)[0, 0] + 1.0
        qf = jnp.asarray(qi, jnp.float32)
        nsteps = jnp.maximum(jnp.where(need < qf, need, qf), 1.0).astype(jnp.int32)

        npair = (nsteps - 1) // 2

        def pair(i, c):
            t = 2 * i + 1
            issue(t + 1, buf_b)
            step(t, buf_a)
            issue(t + 2, buf_a)
            step(t + 1, buf_b)
            return c

        def two_pairs(i, c):
            return pair(2 * i + 1, pair(2 * i, c))

        lax.fori_loop(0, npair // 2, two_pairs, 0)
        lax.fori_loop(2 * (npair // 2), npair, pair, 0)
        t0 = 2 * npair + 1
        rest = nsteps - 2 * npair

        @pl.when(jnp.logical_and(qi >= 1, rest == 1))
        def _():
            produce_next()
            step(t0, buf_a)

        @pl.when(jnp.logical_and(qi >= 1, rest == 2))
        def _():
            issue(t0 + 1, buf_b)
            step(t0, buf_a)
            produce_next()
            step(t0 + 1, buf_b)

        acc = acc_ref[:ATT_HEAD_DIM, :]
        l = acc_ref[ATT_HEAD_DIM:ATT_HEAD_DIM + 1, :]
        oT = acc[:, :blk] / l[:, :blk] - lam * (acc[:, blk:] / l[:, blk:])
        ms = jnp.mean(oT * oT, axis=0, keepdims=True)
        oT = oT * lax.rsqrt(ms + LN_EPS) * gain_ref[...] * (1.0 - lam_init)
        o_ref[0, pl.ds(pl.multiple_of(qi * blk, blk), blk), :] = oT.T.astype(o_ref.dtype)
        return carry

    lax.fori_loop(0, nq, qblock, 0)


def _diff_attention(slopes2, qT, k, vT, lq1, lk1, lq2, lk2, gain_col, lam_init):
    bsz, nq, _, bq = qT.shape
    _, nk, _, bk = vT.shape
    assert bq == bk and nq == nk
    s = nq * bq
    vec = lambda bi, h: (0, 0)
    head = lambda bi, h: (bi, 0, h, 0)
    return pl.pallas_call(
        functools.partial(_attn_kernel, lam_init=lam_init),
        grid=(bsz, ATT_HEADS),
        in_specs=[
            pl.BlockSpec(memory_space=pltpu.SMEM),
            pl.BlockSpec((1, nq, ATT_HEAD_DIM, bq), head),
            pl.BlockSpec((1, s, ATT_HEAD_DIM), lambda bi, h: (bi, 0, h)),
            pl.BlockSpec((1, nk, ATT_HEAD_DIM, bk), head),
            pl.BlockSpec((1, QK_DIM), vec),
            pl.BlockSpec((1, QK_DIM), vec),
            pl.BlockSpec((1, QK_DIM), vec),
            pl.BlockSpec((1, QK_DIM), vec),
            pl.BlockSpec((ATT_HEAD_DIM, 1), vec),
        ],
        out_specs=pl.BlockSpec((1, s, ATT_HEAD_DIM), lambda bi, h: (bi, 0, h)),
        out_shape=jax.ShapeDtypeStruct((bsz, s, ATT_WIDTH), jnp.bfloat16),
        scratch_shapes=[
            pltpu.VMEM((2 * ATT_HEAD_DIM, 2 * bq), jnp.bfloat16),
            pltpu.VMEM((2 * ATT_HEAD_DIM, 2 * bq), jnp.bfloat16),
            pltpu.VMEM((bk, ATT_HEAD_DIM), jnp.bfloat16),
            pltpu.VMEM((1, 2 * bq), jnp.float32),
            pltpu.VMEM((ATT_HEAD_DIM + ATT_SUM_ROWS, 2 * bq), jnp.float32),
            pltpu.VMEM((bk, 2 * bq), jnp.float32),
            pltpu.VMEM((bk, 2 * bq), jnp.float32),
            pltpu.VMEM((bk, 2 * bq), jnp.float32),
            pltpu.VMEM((1, 2 * bq), jnp.float32),
            pltpu.VMEM((1, 2 * bq), jnp.float32),
            pltpu.VMEM((1, 2 * bq), jnp.float32),
        ],
        compiler_params=pltpu.CompilerParams(
            dimension_semantics=("arbitrary", "arbitrary"),
            vmem_limit_bytes=VMEM_LIMIT),
        name="diff_attn",
    )(slopes2, qT, k, vT, lq1, lk1, lq2, lk2, gain_col)


def _memkv_kernel(mem_ref, w_ref, kc_ref, vc_ref):
    mb = mem_ref[0].astype(jnp.bfloat16)
    kc_ref[0] = _dot(mb, w_ref[:, :D_MODEL]).astype(jnp.bfloat16)
    vc_ref[0] = _dot(mb, w_ref[:, D_MODEL:]).astype(jnp.bfloat16)


def _mem_kv(mem, w_kv):
    bsz, m, d = mem.shape
    return pl.pallas_call(
        _memkv_kernel,
        grid=(bsz,),
        in_specs=[
            pl.BlockSpec((1, m, d), lambda bi: (bi, 0, 0)),
            pl.BlockSpec((d, 2 * d), lambda bi: (0, 0)),
        ],
        out_specs=[
            pl.BlockSpec((1, m, d), lambda bi: (bi, 0, 0)),
            pl.BlockSpec((1, m, d), lambda bi: (bi, 0, 0)),
        ],
        out_shape=[
            jax.ShapeDtypeStruct((bsz, m, d), jnp.bfloat16),
            jax.ShapeDtypeStruct((bsz, m, d), jnp.bfloat16),
        ],
        compiler_params=pltpu.CompilerParams(
            dimension_semantics=("arbitrary",), vmem_limit_bytes=VMEM_LIMIT),
        name="mem_kv",
    )(mem, w_kv)


def _conv_pieces(glu_ref, halo, cw_ref, cb_ref, ng_ref, nb_ref, xx_ref, act_ref):
    tm = POST_TM
    first = CONV_HALO - (CONV_KERNEL - 1)

    def fill():
        for g in range(CONV_GROUPS):
            gs = slice(g * CONV_GROUP_DIM, (g + 1) * CONV_GROUP_DIM)
            xx_ref[g, :CONV_HALO, :] = halo[:, gs]
            xx_ref[g, CONV_HALO:, :] = glu_ref[:, gs]

    def sweep(g, r0):
        gs = slice(g * CONV_GROUP_DIM, (g + 1) * CONV_GROUP_DIM)
        accs = [jnp.broadcast_to(cb_ref[:, gs], (CONV_ROWS, CONV_GROUP_DIM))] * CONV_CHUNKS
        for j in range(CONV_KERNEL):
            wj = jnp.broadcast_to(cw_ref[j:j + 1, gs], (CONV_ROWS, CONV_GROUP_DIM))
            for rc in range(CONV_CHUNKS):
                r = r0 + rc * CONV_ROWS + first + j
                accs[rc] = accs[rc] + wj * xx_ref[g, r:r + CONV_ROWS, :]
        for rc in range(CONV_CHUNKS):
            y = _layer_norm(accs[rc], ng_ref[:, gs], nb_ref[:, gs])
            y = y * (1.0 / (1.0 + jnp.exp(-y)))
            r = r0 + rc * CONV_ROWS
            act_ref[r:r + CONV_ROWS, gs] = y.astype(jnp.bfloat16)

    return [fill] + [functools.partial(sweep, g, r0) for g in range(CONV_GROUPS)
                     for r0 in range(0, tm, CONV_ROWS * CONV_CHUNKS)]


def _post_kernel(xn_ref, att_ref, glu0_ref, glun_ref, halo_ref, kc_ref, vc_ref,
                 cw_ref, cb_ref, ng_ref, nb_ref, wpw_ref, bpw_ref,
                 wo_ref, g1_ref, b1_ref, wq_ref, wom_ref, g2_ref, b2_ref,
                 wg_ref, wu_ref, wd_ref, g3_ref, b3_ref, o_ref,
                 ca_ref, c_ref, xx_ref, act_ref, *, tiles_per_batch):
    bf16 = jnp.bfloat16
    i = pl.program_id(0)

    def pointwise():
        return (_dot(act_ref[...], wpw_ref[...]) + bpw_ref[...]).astype(bf16)

    @pl.when(i == 0)
    def _():
        for piece in _conv_pieces(glu0_ref, jnp.zeros((CONV_HALO, CONV_WIDTH), jnp.float32),
                                  cw_ref, cb_ref, ng_ref, nb_ref, xx_ref, act_ref):
            piece()
        c_ref[...] = pointwise()

    mix = _dot(att_ref[...], wo_ref[:ATT_WIDTH, :]) + _dot(c_ref[...], wo_ref[ATT_WIDTH:, :])

    x1 = _layer_norm(DEEPNORM_ALPHA * xn_ref[...] + mix, g1_ref[...], b1_ref[...])

    halo = halo_ref[...]
    halo = jnp.where((i + 1) % tiles_per_batch == 0, jnp.zeros_like(halo), halo)
    for piece in _conv_pieces(glun_ref, halo, cw_ref, cb_ref, ng_ref, nb_ref, xx_ref, act_ref):
        piece()

    qscale = (MEM_HEAD_DIM ** -0.5) * LOG2E
    qm = (_dot(x1.astype(bf16), wq_ref[...]) * qscale).astype(bf16)
    for h in range(MEM_HEADS):
        hs = slice(h * MEM_HEAD_DIM, (h + 1) * MEM_HEAD_DIM)
        logits = _dot_nt(qm[:, hs], kc_ref[0, :, hs])
        p = jnp.exp2(logits - jnp.max(logits, axis=-1, keepdims=True))
        denom = jnp.sum(p, axis=-1, keepdims=True)
        ca = _dot(p.astype(bf16), vc_ref[0, :, hs]) / denom
        ca_ref[:, hs] = ca.astype(bf16)
    x2 = _layer_norm(DEEPNORM_ALPHA * x1 + _dot(ca_ref[...], wom_ref[...]),
                     g2_ref[...], b2_ref[...])

    x2b = x2.astype(bf16)
    gate = _dot(x2b, wg_ref[...])
    up = _dot(x2b, wu_ref[...])
    hdn = (gate * (1.0 / (1.0 + jnp.exp(-gate))) * up).astype(bf16)
    o_ref[...] = _layer_norm(DEEPNORM_ALPHA * x2 + _dot(hdn, wd_ref[...]), g3_ref[...], b3_ref[...])
    c_ref[...] = pointwise()


def _post(xn, att, glu, kc, vc, cw, cb, ng, nb, wpw, bpw,
          wo, g1, b1, wq, wom, g2, b2, wg, wu, wd, g3, b3, tiles_per_batch):
    t, d = xn.shape
    nt = t // POST_TM
    cw_ = CONV_WIDTH
    halo_blocks = POST_TM // CONV_HALO
    row = lambda i: (i, 0)
    const = lambda i: (0, 0)
    nxt = lambda i: (jnp.minimum(i + 1, nt - 1), 0)
    nxt_halo = lambda i: (jnp.minimum(i + 1, nt - 1) * halo_blocks - 1, 0)

    def resident(shape):
        return pl.BlockSpec(shape, const, pipeline_mode=pl.Buffered(1))

    return pl.pallas_call(
        functools.partial(_post_kernel, tiles_per_batch=tiles_per_batch),
        grid=(nt,),
        in_specs=[
            pl.BlockSpec((POST_TM, d), row),
            pl.BlockSpec((POST_TM, ATT_WIDTH), row),
            resident((POST_TM, cw_)),
            pl.BlockSpec((POST_TM, cw_), nxt),
            pl.BlockSpec((CONV_HALO, cw_), nxt_halo),
            pl.BlockSpec((1, MEM_LEN, d), lambda i: (i // tiles_per_batch, 0, 0)),
            pl.BlockSpec((1, MEM_LEN, d), lambda i: (i // tiles_per_batch, 0, 0)),
            resident((CONV_KERNEL, cw_)), resident((1, cw_)), resident((1, cw_)), resident((1, cw_)),
            resident((cw_, cw_)), resident((1, cw_)),
            resident((d, d)), resident((1, d)), resident((1, d)),
            resident((d, d)), resident((d, d)), resident((1, d)), resident((1, d)),
            resident((d, D_FF)), resident((d, D_FF)), resident((D_FF, d)),
            resident((1, d)), resident((1, d)),
        ],
        out_specs=pl.BlockSpec((POST_TM, d), row),
        out_shape=jax.ShapeDtypeStruct((t, d), jnp.float32),
        scratch_shapes=[
            pltpu.VMEM((POST_TM, d), jnp.bfloat16),
            pltpu.VMEM((POST_TM, cw_), jnp.bfloat16),
            pltpu.VMEM((CONV_GROUPS, CONV_HALO + POST_TM, CONV_GROUP_DIM), jnp.float32),
            pltpu.VMEM((POST_TM, cw_), jnp.bfloat16),
        ],
        compiler_params=pltpu.CompilerParams(
            dimension_semantics=("arbitrary",), vmem_limit_bytes=VMEM_LIMIT),
        name="post",
    )(xn, att, glu, glu, glu, kc, vc, cw, cb, ng, nb, wpw, bpw,
      wo, g1, b1, wq, wom, g2, b2, wg, wu, wd, g3, b3)


def kernel(x, mem, in_norm_g, in_norm_b, w_in, lambda_q1, lambda_k1, lambda_q2, lambda_k2,
           subln_g, conv_w, conv_b, conv_norm_g, conv_norm_b, w_pw, b_pw, w_o, ln1_g, ln1_b,
           w_q_mem, w_kv_mem, w_o_mem, ln2_g, ln2_b, w_gate, w_up, w_down, ln3_g, ln3_b):
    assert DEPTH == 1 and w_in.shape[0] == 1
    bsz, s, d = x.shape
    bf16 = jnp.bfloat16
    row = lambda v: v.reshape(1, -1)
    l = 0
    lam_init = 0.8 - 0.6 * math.exp(-0.3 * l)
    slopes2 = jnp.asarray([_alibi_slope(h) * LOG2E for h in range(ATT_HEADS)], jnp.float32)
    w = w_in[l].astype(bf16)
    wq = w[:, :QK_TOTAL]
    wk = w[:, QK_TOTAL:2 * QK_TOTAL]
    wv = w[:, 2 * QK_TOTAL:2 * QK_TOTAL + ATT_WIDTH]
    wc = w[:, 2 * QK_TOTAL + ATT_WIDTH:]
    xn, qT, k, vT, glu = _in_proj(x, row(in_norm_g), row(in_norm_b), wq, wk, wv, wc)
    att = _diff_attention(slopes2, qT, k, vT, row(lambda_q1[l]), row(lambda_k1[l]),
                          row(lambda_q2[l]), row(lambda_k2[l]),
                          subln_g[l].reshape(-1, 1), lam_init)
    kc, vc = _mem_kv(mem, w_kv_mem[l].astype(bf16))
    out = _post(xn.reshape(bsz * s, d), att.reshape(bsz * s, ATT_WIDTH),
                glu.reshape(bsz * s, CONV_WIDTH), kc, vc,
                conv_w[l], row(conv_b[l]), row(conv_norm_g[l]), row(conv_norm_b[l]),
                w_pw[l].astype(bf16), row(b_pw[l]),
                w_o[l].astype(bf16), row(ln1_g[l]), row(ln1_b[l]),
                w_q_mem[l].astype(bf16), w_o_mem[l].astype(bf16), row(ln2_g[l]), row(ln2_b[l]),
                w_gate[l].astype(bf16), w_up[l].astype(bf16), w_down[l].astype(bf16),
                row(ln3_g[l]), row(ln3_b[l]), s // POST_TM)
    return out.reshape(bsz, s, d)
```

```python
import functools
import math

import jax
import jax.numpy as jnp
from jax import lax
from jax.experimental import pallas as pl
from jax.experimental.pallas import tpu as pltpu

D_MODEL = 1024
DEPTH = 1
MEM_LEN = 256
ATT_WIDTH = 512
CONV_WIDTH = 512
ATT_HEADS = 4
ATT_HEAD_DIM = 128
QK_DIM = 64
QK_TOTAL = 512
CONV_GROUPS = 4
CONV_GROUP_DIM = 128
CONV_KERNEL = 31
MEM_HEADS = 4
MEM_HEAD_DIM = 256
D_FF = 2816
LN_EPS = 1e-5
DEEPNORM_ALPHA = (2 * DEPTH) ** 0.25

LOG2E = 1.4426950408889634
NEG_BIG = -1e30
SKIP_LOG2 = 152.0

ATT_BQ = 512
ATT_BK = 512
ATT_SUM_ROWS = 16
PROJ_TM = 512
CONV_HALO = 32
CONV_ROWS = 64
CONV_CHUNKS = 2
POST_TM = 512

VMEM_LIMIT = 60 * 1024 * 1024

_NT = (((1,), (1,)), ((), ()))


def _layer_norm(x, g, b):
    mu = jnp.mean(x, axis=-1, keepdims=True)
    xc = x - mu
    var = jnp.mean(xc * xc, axis=-1, keepdims=True)
    return xc * lax.rsqrt(var + LN_EPS) * g + b


def _alibi_slope(h):
    return 2.0 ** (-8.0 * (h + 1) / ATT_HEADS)


def _dot(a, b):
    return jnp.dot(a, b, preferred_element_type=jnp.float32)


def _dot_nt(a, b):
    return lax.dot_general(a, b, _NT, preferred_element_type=jnp.float32)


def _inproj_kernel(x_ref, g_ref, b_ref, wq_ref, wk_ref, wv_ref, wc_ref,
                   xn_ref, qT_ref, k_ref, vT_ref, glu_ref):
    tm = PROJ_TM
    xn = _layer_norm(x_ref[0], g_ref[...], b_ref[...])
    xn_ref[0] = xn
    xb = xn.astype(jnp.bfloat16)
    k_ref[0] = _dot(xb, wk_ref[...]).astype(jnp.bfloat16)
    qscale = (QK_DIM ** -0.5) * LOG2E
    qT = (_dot(xb, wq_ref[...]) * qscale).T
    vT = _dot(xb, wv_ref[...]).T
    for j in range(tm // ATT_BQ):
        qT_ref[0, j] = qT[:, j * ATT_BQ:(j + 1) * ATT_BQ].astype(jnp.bfloat16)
    for j in range(tm // ATT_BK):
        vT_ref[0, j] = vT[:, j * ATT_BK:(j + 1) * ATT_BK].astype(jnp.bfloat16)
    c_val = _dot(xb, wc_ref[:, :CONV_WIDTH])
    c_gate = _dot(xb, wc_ref[:, CONV_WIDTH:])
    glu_ref[0] = c_val * (1.0 / (1.0 + jnp.exp(-c_gate)))


def _in_proj(x, g, b, wq, wk, wv, wc):
    bsz, s, d = x.shape
    nt = s // PROJ_TM
    nbq = PROJ_TM // ATT_BQ
    nbk = PROJ_TM // ATT_BK
    const = lambda bi, si: (0, 0)
    return pl.pallas_call(
        _inproj_kernel,
        grid=(bsz, nt),
        in_specs=[
            pl.BlockSpec((1, PROJ_TM, d), lambda bi, si: (bi, si, 0)),
            pl.BlockSpec((1, d), const),
            pl.BlockSpec((1, d), const),
            pl.BlockSpec((d, QK_TOTAL), const),
            pl.BlockSpec((d, QK_TOTAL), const),
            pl.BlockSpec((d, ATT_WIDTH), const),
            pl.BlockSpec((d, 2 * CONV_WIDTH), const),
        ],
        out_specs=[
            pl.BlockSpec((1, PROJ_TM, d), lambda bi, si: (bi, si, 0)),
            pl.BlockSpec((1, nbq, QK_TOTAL, ATT_BQ), lambda bi, si: (bi, si, 0, 0)),
            pl.BlockSpec((1, PROJ_TM, QK_TOTAL), lambda bi, si: (bi, si, 0)),
            pl.BlockSpec((1, nbk, ATT_WIDTH, ATT_BK), lambda bi, si: (bi, si, 0, 0)),
            pl.BlockSpec((1, PROJ_TM, CONV_WIDTH), lambda bi, si: (bi, si, 0)),
        ],
        out_shape=[
            jax.ShapeDtypeStruct((bsz, s, d), jnp.float32),
            jax.ShapeDtypeStruct((bsz, s // ATT_BQ, QK_TOTAL, ATT_BQ), jnp.bfloat16),
            jax.ShapeDtypeStruct((bsz, s, QK_TOTAL), jnp.bfloat16),
            jax.ShapeDtypeStruct((bsz, s // ATT_BK, ATT_WIDTH, ATT_BK), jnp.bfloat16),
            jax.ShapeDtypeStruct((bsz, s, CONV_WIDTH), jnp.float32),
        ],
        compiler_params=pltpu.CompilerParams(
            dimension_semantics=("arbitrary", "arbitrary"),
            vmem_limit_bytes=VMEM_LIMIT),
        name="in_proj",
    )(x, g, b, wq, wk, wv, wc)


def _attn_kernel(slope_ref, qT_ref, k_ref, vT_ref, lq1_ref, lk1_ref, lq2_ref, lk2_ref,
                 gain_ref, o_ref, qs_ref, qsn_ref, kbias_ref, m_ref, acc_ref,
                 sa_ref, sb_ref, sc_ref, mxa_ref, mxb_ref, mxc_ref, *, lam_init):
    blk = ATT_BQ
    nq = qT_ref.shape[1]
    slope2 = slope_ref[pl.program_id(1)]

    pos = lax.broadcasted_iota(jnp.int32, (blk, ATT_HEAD_DIM), 0)
    lane = lax.broadcasted_iota(jnp.int32, (blk, ATT_HEAD_DIM), 1)
    u = slope2 * pos.astype(jnp.float32)
    hi = u.astype(jnp.bfloat16).astype(jnp.float32)
    mid = (u - hi).astype(jnp.bfloat16).astype(jnp.float32)
    lo = u - hi - mid
    cols = jnp.where(lane == 0, hi, jnp.where(lane == 1, mid, jnp.where(lane == 2, lo, 0.0)))
    kbias_ref[...] = cols.astype(jnp.bfloat16)

    row2 = lax.broadcasted_iota(jnp.int32, (ATT_HEAD_DIM, 2 * blk), 0)
    ones3 = jnp.where(row2 < 3, 1.0, 0.0).astype(jnp.bfloat16)
    qs_ref[ATT_HEAD_DIM:, :] = ones3
    qsn_ref[ATT_HEAD_DIM:, :] = ones3

    def stack_q(dst_ref, qidx):
        qT = qT_ref[0, qidx]
        row = lax.broadcasted_iota(jnp.int32, (ATT_HEAD_DIM, blk), 0)
        zero = jnp.zeros_like(qT)
        dst_ref[:ATT_HEAD_DIM, :blk] = jnp.where(row < QK_DIM, qT, zero)
        dst_ref[:ATT_HEAD_DIM, blk:] = jnp.where(row >= QK_DIM, qT, zero)

    ones_rows = jnp.ones((ATT_SUM_ROWS, blk), jnp.bfloat16)

    def produce(q_ref, qi, j, buf, masked):
        s_ref, mx_ref = buf
        start = j * blk if isinstance(j, int) else pl.multiple_of(j * blk, blk)
        kb = jnp.concatenate([k_ref[0, pl.ds(start, blk), :], kbias_ref[...]], axis=1)
        s = _dot(kb, q_ref[...])
        if masked:
            kr = lax.broadcasted_iota(jnp.int32, (blk, blk), 0)
            qc = lax.broadcasted_iota(jnp.int32, (blk, blk), 1)
            valid = (kr - qc) <= (qi - j) * blk
            s = jnp.where(jnp.concatenate([valid, valid], axis=1), s, NEG_BIG)
        s_ref[...] = s
        mx_ref[...] = jnp.max(s, axis=0, keepdims=True)

    def consume(qi, j, buf):
        s_ref, mx_ref = buf
        off = slope2 * jnp.asarray((j - qi) * blk, jnp.float32)
        m_old = m_ref[...]
        m_new = jnp.maximum(m_old, mx_ref[...] + off)
        p = jnp.exp2(s_ref[...] - (m_new - off)).astype(jnp.bfloat16)
        alpha = jnp.exp2(m_old - m_new)
        lhs = jnp.concatenate([vT_ref[0, j], ones_rows], axis=0)
        acc_ref[...] = alpha * acc_ref[...] + _dot(lhs, p)
        m_ref[...] = m_new

    lam = (jnp.exp(jnp.sum(lq1_ref[...] * lk1_ref[...], axis=-1, keepdims=True))
           - jnp.exp(jnp.sum(lq2_ref[...] * lk2_ref[...], axis=-1, keepdims=True))
           + lam_init)

    kabs = jnp.max(jnp.abs(k_ref[0].astype(jnp.float32)), axis=0, keepdims=True)
    row8 = lax.broadcasted_iota(jnp.int32, (8, ATT_HEAD_DIM), 0)
    kabs8 = jnp.where(row8 == 0, kabs, 0.0).astype(jnp.bfloat16)
    inv_chunk_bias = 1.0 / (slope2 * blk)

    buf_a, buf_b, buf_c = (sa_ref, mxa_ref), (sb_ref, mxb_ref), (sc_ref, mxc_ref)

    stack_q(qsn_ref, 0)
    produce(qsn_ref, 0, 0, buf_c, True)

    def qblock(qi, carry):
        qnext = jnp.minimum(qi + 1, nq - 1)
        stack_q(qs_ref, qi)
        stack_q(qsn_ref, qnext)
        m_ref[...] = jnp.full(m_ref.shape, NEG_BIG, jnp.float32)
        acc_ref[...] = jnp.zeros(acc_ref.shape, jnp.float32)

        def step(t, buf):
            consume(qi, qi - t, buf)

        def issue(t, buf):
            produce(qs_ref, qi, qi - t, buf, False)

        def produce_next():
            produce(qsn_ref, qnext, qnext, buf_c, True)

        score_bound = _dot(kabs8, jnp.abs(qs_ref[:ATT_HEAD_DIM, :]))[:1, :]
        gap = jnp.max(score_bound - mxc_ref[...], axis=1, keepdims=True) + SKIP_LOG2
        need = jnp.floor(gap * inv_chunk_bias)[0, 0] + 1.0
        qf = jnp.asarray(qi, jnp.float32)
        nsteps = jnp.maximum(jnp.where(need < qf, need, qf), 1.0).astype(jnp.int32)

        produce(qs_ref, qi, jnp.maximum(qi - 1, 0), buf_a, False)
        step(0, buf_c)

        npair = (nsteps - 1) // 2

        def pair(i, c):
            t = 2 * i + 1
            issue(t + 1, buf_b)
            step(t, buf_a)
            issue(t + 2, buf_a)
            step(t + 1, buf_b)
            return c

        def two_pairs(i, c):
            return pair(2 * i + 1, pair(2 * i, c))

        lax.fori_loop(0, npair // 2, two_pairs, 0)
        lax.fori_loop(2 * (npair // 2), npair, pair, 0)
        t0 = 2 * npair + 1
        rest = nsteps - 2 * npair

        @pl.when(qi == 0)
        def _():
            produce_next()

        @pl.when(jnp.logical_and(qi >= 1, rest == 1))
        def _():
            produce_next()
            step(t0, buf_a)

        @pl.when(jnp.logical_and(qi >= 1, rest == 2))
        def _():
            issue(t0 + 1, buf_b)
            step(t0, buf_a)
            produce_next()
            step(t0 + 1, buf_b)

        acc = acc_ref[:ATT_HEAD_DIM, :]
        l = acc_ref[ATT_HEAD_DIM:ATT_HEAD_DIM + 1, :]
        oT = acc[:, :blk] / l[:, :blk] - lam * (acc[:, blk:] / l[:, blk:])
        ms = jnp.mean(oT * oT, axis=0, keepdims=True)
        oT = oT * lax.rsqrt(ms + LN_EPS) * gain_ref[...] * (1.0 - lam_init)
        o_ref[0, pl.ds(pl.multiple_of(qi * blk, blk), blk), :] = oT.T.astype(o_ref.dtype)
        return carry

    lax.fori_loop(0, nq, qblock, 0)


def _diff_attention(slopes2, qT, k, vT, lq1, lk1, lq2, lk2, gain_col, lam_init):
    bsz, nq, _, bq = qT.shape
    _, nk, _, bk = vT.shape
    assert bq == bk and nq == nk
    s = nq * bq
    vec = lambda bi, h: (0, 0)
    head = lambda bi, h: (bi, 0, h, 0)
    return pl.pallas_call(
        functools.partial(_attn_kernel, lam_init=lam_init),
        grid=(bsz, ATT_HEADS),
        in_specs=[
            pl.BlockSpec(memory_space=pltpu.SMEM),
            pl.BlockSpec((1, nq, ATT_HEAD_DIM, bq), head),
            pl.BlockSpec((1, s, ATT_HEAD_DIM), lambda bi, h: (bi, 0, h)),
            pl.BlockSpec((1, nk, ATT_HEAD_DIM, bk), head),
            pl.BlockSpec((1, QK_DIM), vec),
            pl.BlockSpec((1, QK_DIM), vec),
            pl.BlockSpec((1, QK_DIM), vec),
            pl.BlockSpec((1, QK_DIM), vec),
            pl.BlockSpec((ATT_HEAD_DIM, 1), vec),
        ],
        out_specs=pl.BlockSpec((1, s, ATT_HEAD_DIM), lambda bi, h: (bi, 0, h)),
        out_shape=jax.ShapeDtypeStruct((bsz, s, ATT_WIDTH), jnp.bfloat16),
        scratch_shapes=[
            pltpu.VMEM((2 * ATT_HEAD_DIM, 2 * bq), jnp.bfloat16),
            pltpu.VMEM((2 * ATT_HEAD_DIM, 2 * bq), jnp.bfloat16),
            pltpu.VMEM((bk, ATT_HEAD_DIM), jnp.bfloat16),
            pltpu.VMEM((1, 2 * bq), jnp.float32),
            pltpu.VMEM((ATT_HEAD_DIM + ATT_SUM_ROWS, 2 * bq), jnp.float32),
            pltpu.VMEM((bk, 2 * bq), jnp.float32),
            pltpu.VMEM((bk, 2 * bq), jnp.float32),
            pltpu.VMEM((bk, 2 * bq), jnp.float32),
            pltpu.VMEM((1, 2 * bq), jnp.float32),
            pltpu.VMEM((1, 2 * bq), jnp.float32),
            pltpu.VMEM((1, 2 * bq), jnp.float32),
        ],
        compiler_params=pltpu.CompilerParams(
            dimension_semantics=("arbitrary", "arbitrary"),
            vmem_limit_bytes=VMEM_LIMIT),
        name="diff_attn",
    )(slopes2, qT, k, vT, lq1, lk1, lq2, lk2, gain_col)


def _memkv_kernel(mem_ref, w_ref, kc_ref, vc_ref):
    mb = mem_ref[0].astype(jnp.bfloat16)
    kc_ref[0] = _dot(mb, w_ref[:, :D_MODEL]).astype(jnp.bfloat16)
    vc_ref[0] = _dot(mb, w_ref[:, D_MODEL:]).astype(jnp.bfloat16)


def _mem_kv(mem, w_kv):
    bsz, m, d = mem.shape
    return pl.pallas_call(
        _memkv_kernel,
        grid=(bsz,),
        in_specs=[
            pl.BlockSpec((1, m, d), lambda bi: (bi, 0, 0)),
            pl.BlockSpec((d, 2 * d), lambda bi: (0, 0)),
        ],
        out_specs=[
            pl.BlockSpec((1, m, d), lambda bi: (bi, 0, 0)),
            pl.BlockSpec((1, m, d), lambda bi: (bi, 0, 0)),
        ],
        out_shape=[
            jax.ShapeDtypeStruct((bsz, m, d), jnp.bfloat16),
            jax.ShapeDtypeStruct((bsz, m, d), jnp.bfloat16),
        ],
        compiler_params=pltpu.CompilerParams(
            dimension_semantics=("arbitrary",), vmem_limit_bytes=VMEM_LIMIT),
        name="mem_kv",
    )(mem, w_kv)


def _conv_pieces(glu_ref, halo, cw_ref, cb_ref, ng_ref, nb_ref, xx_ref, act_ref):
    tm = POST_TM
    first = CONV_HALO - (CONV_KERNEL - 1)

    def fill():
        for g in range(CONV_GROUPS):
            gs = slice(g * CONV_GROUP_DIM, (g + 1) * CONV_GROUP_DIM)
            xx_ref[g, :CONV_HALO, :] = halo[:, gs]
            xx_ref[g, CONV_HALO:, :] = glu_ref[:, gs]

    def sweep(g, r0):
        gs = slice(g * CONV_GROUP_DIM, (g + 1) * CONV_GROUP_DIM)
        accs = [jnp.broadcast_to(cb_ref[:, gs], (CONV_ROWS, CONV_GROUP_DIM))] * CONV_CHUNKS
        for j in range(CONV_KERNEL):
            wj = jnp.broadcast_to(cw_ref[j:j + 1, gs], (CONV_ROWS, CONV_GROUP_DIM))
            for rc in range(CONV_CHUNKS):
                r = r0 + rc * CONV_ROWS + first + j
                accs[rc] = accs[rc] + wj * xx_ref[g, r:r + CONV_ROWS, :]
        for rc in range(CONV_CHUNKS):
            y = _layer_norm(accs[rc], ng_ref[:, gs], nb_ref[:, gs])
            y = y * (1.0 / (1.0 + jnp.exp(-y)))
            r = r0 + rc * CONV_ROWS
            act_ref[r:r + CONV_ROWS, gs] = y.astype(jnp.bfloat16)

    return [fill] + [functools.partial(sweep, g, r0) for g in range(CONV_GROUPS)
                     for r0 in range(0, tm, CONV_ROWS * CONV_CHUNKS)]


def _post_kernel(xn_ref, att_ref, glu0_ref, glun_ref, halo_ref, kc_ref, vc_ref,
                 cw_ref, cb_ref, ng_ref, nb_ref, wpw_ref, bpw_ref,
                 wo_ref, g1_ref, b1_ref, wq_ref, wom_ref, g2_ref, b2_ref,
                 wg_ref, wu_ref, wd_ref, g3_ref, b3_ref, o_ref,
                 ca_ref, c_ref, xx_ref, act_ref, *, tiles_per_batch):
    bf16 = jnp.bfloat16
    i = pl.program_id(0)

    def pointwise():
        return (_dot(act_ref[...], wpw_ref[...]) + bpw_ref[...]).astype(bf16)

    @pl.when(i == 0)
    def _():
        for piece in _conv_pieces(glu0_ref, jnp.zeros((CONV_HALO, CONV_WIDTH), jnp.float32),
                                  cw_ref, cb_ref, ng_ref, nb_ref, xx_ref, act_ref):
            piece()
        c_ref[...] = pointwise()

    mix = _dot(att_ref[...], wo_ref[:ATT_WIDTH, :]) + _dot(c_ref[...], wo_ref[ATT_WIDTH:, :])

    x1 = _layer_norm(DEEPNORM_ALPHA * xn_ref[...] + mix, g1_ref[...], b1_ref[...])

    halo = halo_ref[...]
    halo = jnp.where((i + 1) % tiles_per_batch == 0, jnp.zeros_like(halo), halo)
    for piece in _conv_pieces(glun_ref, halo, cw_ref, cb_ref, ng_ref, nb_ref, xx_ref, act_ref):
        piece()

    qscale = (MEM_HEAD_DIM ** -0.5) * LOG2E
    qm = (_dot(x1.astype(bf16), wq_ref[...]) * qscale).astype(bf16)
    for h in range(MEM_HEADS):
        hs = slice(h * MEM_HEAD_DIM, (h + 1) * MEM_HEAD_DIM)
        logits = _dot_nt(qm[:, hs], kc_ref[0, :, hs])
        p = jnp.exp2(logits - jnp.max(logits, axis=-1, keepdims=True))
        denom = jnp.sum(p, axis=-1, keepdims=True)
        ca = _dot(p.astype(bf16), vc_ref[0, :, hs]) / denom
        ca_ref[:, hs] = ca.astype(bf16)
    x2 = _layer_norm(DEEPNORM_ALPHA * x1 + _dot(ca_ref[...], wom_ref[...]),
                     g2_ref[...], b2_ref[...])

    x2b = x2.astype(bf16)
    gate = _dot(x2b, wg_ref[...])
    up = _dot(x2b, wu_ref[...])
    hdn = (gate * (1.0 / (1.0 + jnp.exp(-gate))) * up).astype(bf16)
    o_ref[...] = _layer_norm(DEEPNORM_ALPHA * x2 + _dot(hdn, wd_ref[...]), g3_ref[...], b3_ref[...])
    c_ref[...] = pointwise()


def _post(xn, att, glu, kc, vc, cw, cb, ng, nb, wpw, bpw,
          wo, g1, b1, wq, wom, g2, b2, wg, wu, wd, g3, b3, tiles_per_batch):
    t, d = xn.shape
    nt = t // POST_TM
    cw_ = CONV_WIDTH
    halo_blocks = POST_TM // CONV_HALO
    row = lambda i: (i, 0)
    const = lambda i: (0, 0)
    nxt = lambda i: (jnp.minimum(i + 1, nt - 1), 0)
    nxt_halo = lambda i: (jnp.minimum(i + 1, nt - 1) * halo_blocks - 1, 0)

    def resident(shape):
        return pl.BlockSpec(shape, const, pipeline_mode=pl.Buffered(1))

    return pl.pallas_call(
        functools.partial(_post_kernel, tiles_per_batch=tiles_per_batch),
        grid=(nt,),
        in_specs=[
            pl.BlockSpec((POST_TM, d), row),
            pl.BlockSpec((POST_TM, ATT_WIDTH), row),
            resident((POST_TM, cw_)),
            pl.BlockSpec((POST_TM, cw_), nxt),
            pl.BlockSpec((CONV_HALO, cw_), nxt_halo),
            pl.BlockSpec((1, MEM_LEN, d), lambda i: (i // tiles_per_batch, 0, 0)),
            pl.BlockSpec((1, MEM_LEN, d), lambda i: (i // tiles_per_batch, 0, 0)),
            resident((CONV_KERNEL, cw_)), resident((1, cw_)), resident((1, cw_)), resident((1, cw_)),
            resident((cw_, cw_)), resident((1, cw_)),
            resident((d, d)), resident((1, d)), resident((1, d)),
            resident((d, d)), resident((d, d)), resident((1, d)), resident((1, d)),
            resident((d, D_FF)), resident((d, D_FF)), resident((D_FF, d)),
            resident((1, d)), resident((1, d)),
        ],
        out_specs=pl.BlockSpec((POST_TM, d), row),
        out_shape=jax.ShapeDtypeStruct((t, d), jnp.float32),
        scratch_shapes=[
            pltpu.VMEM((POST_TM, d), jnp.bfloat16),
            pltpu.VMEM((POST_TM, cw_), jnp.bfloat16),
            pltpu.VMEM((CONV_GROUPS, CONV_HALO + POST_TM, CONV_GROUP_DIM), jnp.float32),
            pltpu.VMEM((POST_TM, cw_), jnp.bfloat16),
        ],
        compiler_params=pltpu.CompilerParams(
            dimension_semantics=("arbitrary",), vmem_limit_bytes=VMEM_LIMIT),
        name="post",
    )(xn, att, glu, glu, glu, kc, vc, cw, cb, ng, nb, wpw, bpw,
      wo, g1, b1, wq, wom, g2, b2, wg, wu, wd, g3, b3)


def kernel(x, mem, in_norm_g, in_norm_b, w_in, lambda_q1, lambda_k1, lambda_q2, lambda_k2,
           subln_g, conv_w, conv_b, conv_norm_g, conv_norm_b, w_pw, b_pw, w_o, ln1_g, ln1_b,
           w_q_mem, w_kv_mem, w_o_mem, ln2_g, ln2_b, w_gate, w_up, w_down, ln3_g, ln3_b):
    assert DEPTH == 1 and w_in.shape[0] == 1
    bsz, s, d = x.shape
    bf16 = jnp.bfloat16
    row = lambda v: v.reshape(1, -1)
    l = 0
    lam_init = 0.8 - 0.6 * math.exp(-0.3 * l)
    slopes2 = jnp.asarray([_alibi_slope(h) * LOG2E for h in range(ATT_HEADS)], jnp.float32)
    w = w_in[l].astype(bf16)
    wq = w[:, :QK_TOTAL]
    wk = w[:, QK_TOTAL:2 * QK_TOTAL]
    wv = w[:, 2 * QK_TOTAL:2 * QK_TOTAL + ATT_WIDTH]
    wc = w[:, 2 * QK_TOTAL + ATT_WIDTH:]
    xn, qT, k, vT, glu = _in_proj(x, row(in_norm_g), row(in_norm_b), wq, wk, wv, wc)
    att = _diff_attention(slopes2, qT, k, vT, row(lambda_q1[l]), row(lambda_k1[l]),
                          row(lambda_q2[l]), row(lambda_k2[l]),
                          subln_g[l].reshape(-1, 1), lam_init)
    kc, vc = _mem_kv(mem, w_kv_mem[l].astype(bf16))
    out = _post(xn.reshape(bsz * s, d), att.reshape(bsz * s, ATT_WIDTH),
                glu.reshape(bsz * s, CONV_WIDTH), kc, vc,
                conv_w[l], row(conv_b[l]), row(conv_norm_g[l]), row(conv_norm_b[l]),
                w_pw[l].astype(bf16), row(b_pw[l]),
                w_o[l].astype(bf16), row(ln1_g[l]), row(ln1_b[l]),
                w_q_mem[l].astype(bf16), w_o_mem[l].astype(bf16), row(ln2_g[l]), row(ln2_b[l]),
                w_gate[l].astype(bf16), w_up[l].astype(bf16), w_down[l].astype(bf16),
                row(ln3_g[l]), row(ln3_b[l]), s // POST_TM)
    return out.reshape(bsz, s, d)
```

```python
import functools
import math

import jax
import jax.numpy as jnp
from jax import lax
from jax.experimental import pallas as pl
from jax.experimental.pallas import tpu as pltpu

D_MODEL = 1024
DEPTH = 1
MEM_LEN = 256
ATT_WIDTH = 512
CONV_WIDTH = 512
ATT_HEADS = 4
ATT_HEAD_DIM = 128
QK_DIM = 64
QK_TOTAL = 512
CONV_GROUPS = 4
CONV_GROUP_DIM = 128
CONV_KERNEL = 31
MEM_HEADS = 4
MEM_HEAD_DIM = 256
D_FF = 2816
LN_EPS = 1e-5
DEEPNORM_ALPHA = (2 * DEPTH) ** 0.25

LOG2E = 1.4426950408889634
NEG_BIG = -1e30
SKIP_LOG2 = 152.0

ATT_BQ = 512
ATT_BK = 512
ATT_SUM_ROWS = 16
PROJ_TM = 512
CONV_HALO = 32
CONV_ROWS = 64
CONV_CHUNKS = 2
POST_TM = 512

VMEM_LIMIT = 60 * 1024 * 1024

_NT = (((1,), (1,)), ((), ()))


def _layer_norm(x, g, b):
    mu = jnp.mean(x, axis=-1, keepdims=True)
    xc = x - mu
    var = jnp.mean(xc * xc, axis=-1, keepdims=True)
    return xc * lax.rsqrt(var + LN_EPS) * g + b


def _alibi_slope(h):
    return 2.0 ** (-8.0 * (h + 1) / ATT_HEADS)


def _dot(a, b):
    return jnp.dot(a, b, preferred_element_type=jnp.float32)


def _dot_nt(a, b):
    return lax.dot_general(a, b, _NT, preferred_element_type=jnp.float32)


def _inproj_kernel(x_ref, g_ref, b_ref, w_ref,
                   xn_ref, qT_ref, k_ref, vT_ref, glu_ref, wb_ref):
    tm = PROJ_TM
    half = tm // 2
    qscale = (QK_DIM ** -0.5) * LOG2E

    @pl.when(jnp.logical_and(pl.program_id(0) == 0, pl.program_id(1) == 0))
    def _():
        wb_ref[...] = w_ref[...].astype(jnp.bfloat16)

    wq_ref = wb_ref.at[:, :QK_TOTAL]
    wk_ref = wb_ref.at[:, QK_TOTAL:2 * QK_TOTAL]
    wv_ref = wb_ref.at[:, 2 * QK_TOTAL:2 * QK_TOTAL + ATT_WIDTH]
    wc_ref = wb_ref.at[:, 2 * QK_TOTAL + ATT_WIDTH:]
    for r in (slice(0, half), slice(half, tm)):
        xn = _layer_norm(x_ref[0, r, :], g_ref[...], b_ref[...])
        xn_ref[0, r, :] = xn
        xb = xn.astype(jnp.bfloat16)
        k_ref[0, r, :] = _dot(xb, wk_ref[...]).astype(jnp.bfloat16)
        qT_ref[0, 0, :, r] = (_dot(xb, wq_ref[...]) * qscale).T.astype(jnp.bfloat16)
        vT_ref[0, 0, :, r] = _dot(xb, wv_ref[...]).T.astype(jnp.bfloat16)
        c_val = _dot(xb, wc_ref[:, :CONV_WIDTH])
        c_gate = _dot(xb, wc_ref[:, CONV_WIDTH:])
        glu_ref[0, r, :] = c_val * (1.0 / (1.0 + jnp.exp(-c_gate)))


def _in_proj(x, g, b, w):
    bsz, s, d = x.shape
    nt = s // PROJ_TM
    nbq = PROJ_TM // ATT_BQ
    nbk = PROJ_TM // ATT_BK
    const = lambda bi, si: (0, 0)
    return pl.pallas_call(
        _inproj_kernel,
        grid=(bsz, nt),
        in_specs=[
            pl.BlockSpec((1, PROJ_TM, d), lambda bi, si: (bi, si, 0)),
            pl.BlockSpec((1, d), const),
            pl.BlockSpec((1, d), const),
            pl.BlockSpec(w.shape, const, pipeline_mode=pl.Buffered(1)),
        ],
        out_specs=[
            pl.BlockSpec((1, PROJ_TM, d), lambda bi, si: (bi, si, 0)),
            pl.BlockSpec((1, nbq, QK_TOTAL, ATT_BQ), lambda bi, si: (bi, si, 0, 0)),
            pl.BlockSpec((1, PROJ_TM, QK_TOTAL), lambda bi, si: (bi, si, 0)),
            pl.BlockSpec((1, nbk, ATT_WIDTH, ATT_BK), lambda bi, si: (bi, si, 0, 0)),
            pl.BlockSpec((1, PROJ_TM, CONV_WIDTH), lambda bi, si: (bi, si, 0)),
        ],
        out_shape=[
            jax.ShapeDtypeStruct((bsz, s, d), jnp.float32),
            jax.ShapeDtypeStruct((bsz, s // ATT_BQ, QK_TOTAL, ATT_BQ), jnp.bfloat16),
            jax.ShapeDtypeStruct((bsz, s, QK_TOTAL), jnp.bfloat16),
            jax.ShapeDtypeStruct((bsz, s // ATT_BK, ATT_WIDTH, ATT_BK), jnp.bfloat16),
            jax.ShapeDtypeStruct((bsz, s, CONV_WIDTH), jnp.float32),
        ],
        scratch_shapes=[pltpu.VMEM(w.shape, jnp.bfloat16)],
        compiler_params=pltpu.CompilerParams(
            dimension_semantics=("arbitrary", "arbitrary"),
            vmem_limit_bytes=VMEM_LIMIT),
        name="in_proj",
    )(x, g, b, w)


def _attn_kernel(slope_ref, qT_ref, k_ref, vT_ref, lq1_ref, lk1_ref, lq2_ref, lk2_ref,
                 gain_ref, o_ref, qs_ref, qsn_ref, kbias_ref, m_ref, acc_ref,
                 sa_ref, sb_ref, sc_ref, mxa_ref, mxb_ref, mxc_ref, *, lam_init):
    blk = ATT_BQ
    nq = qT_ref.shape[1]
    slope2 = slope_ref[pl.program_id(1)]

    pos = lax.broadcasted_iota(jnp.int32, (blk, ATT_HEAD_DIM), 0)
    lane = lax.broadcasted_iota(jnp.int32, (blk, ATT_HEAD_DIM), 1)
    u = slope2 * pos.astype(jnp.float32)
    hi = u.astype(jnp.bfloat16).astype(jnp.float32)
    mid = (u - hi).astype(jnp.bfloat16).astype(jnp.float32)
    lo = u - hi - mid
    cols = jnp.where(lane == 0, hi, jnp.where(lane == 1, mid, jnp.where(lane == 2, lo, 0.0)))
    kbias_ref[...] = cols.astype(jnp.bfloat16)

    row2 = lax.broadcasted_iota(jnp.int32, (ATT_HEAD_DIM, 2 * blk), 0)
    ones3 = jnp.where(row2 < 3, 1.0, 0.0).astype(jnp.bfloat16)
    qs_ref[ATT_HEAD_DIM:, :] = ones3
    qsn_ref[ATT_HEAD_DIM:, :] = ones3

    def stack_q(dst_ref, qidx):
        qT = qT_ref[0, qidx]
        row = lax.broadcasted_iota(jnp.int32, (ATT_HEAD_DIM, blk), 0)
        zero = jnp.zeros_like(qT)
        dst_ref[:ATT_HEAD_DIM, :blk] = jnp.where(row < QK_DIM, qT, zero)
        dst_ref[:ATT_HEAD_DIM, blk:] = jnp.where(row >= QK_DIM, qT, zero)

    ones_rows = jnp.ones((ATT_SUM_ROWS, blk), jnp.bfloat16)

    def produce(q_ref, qi, j, buf, masked):
        s_ref, mx_ref = buf
        start = j * blk if isinstance(j, int) else pl.multiple_of(j * blk, blk)
        kb = jnp.concatenate([k_ref[0, pl.ds(start, blk), :], kbias_ref[...]], axis=1)
        s = _dot(kb, q_ref[...])
        if masked:
            kr = lax.broadcasted_iota(jnp.int32, (blk, blk), 0)
            qc = lax.broadcasted_iota(jnp.int32, (blk, blk), 1)
            valid = (kr - qc) <= (qi - j) * blk
            s = jnp.where(jnp.concatenate([valid, valid], axis=1), s, NEG_BIG)
        s_ref[...] = s
        mx_ref[...] = jnp.max(s, axis=0, keepdims=True)

    def consume(qi, j, buf):
        s_ref, mx_ref = buf
        off = slope2 * jnp.asarray((j - qi) * blk, jnp.float32)
        m_old = m_ref[...]
        m_new = jnp.maximum(m_old, mx_ref[...] + off)
        p = jnp.exp2(s_ref[...] - (m_new - off)).astype(jnp.bfloat16)
        alpha = jnp.exp2(m_old - m_new)
        lhs = jnp.concatenate([vT_ref[0, j], ones_rows], axis=0)
        acc_ref[...] = alpha * acc_ref[...] + _dot(lhs, p)
        m_ref[...] = m_new

    lam = (jnp.exp(jnp.sum(lq1_ref[...] * lk1_ref[...], axis=-1, keepdims=True))
           - jnp.exp(jnp.sum(lq2_ref[...] * lk2_ref[...], axis=-1, keepdims=True))
           + lam_init)

    kabs = jnp.max(jnp.abs(k_ref[0].astype(jnp.float32)), axis=0, keepdims=True)
    row8 = lax.broadcasted_iota(jnp.int32, (8, ATT_HEAD_DIM), 0)
    kabs8 = jnp.where(row8 == 0, kabs, 0.0).astype(jnp.bfloat16)
    inv_chunk_bias = 1.0 / (slope2 * blk)

    buf_a, buf_b, buf_c = (sa_ref, mxa_ref), (sb_ref, mxb_ref), (sc_ref, mxc_ref)

    stack_q(qsn_ref, 0)
    produce(qsn_ref, 0, 0, buf_c, True)

    def qblock(qi, carry):
        qnext = jnp.minimum(qi + 1, nq - 1)
        stack_q(qs_ref, qi)
        stack_q(qsn_ref, qnext)
        m_ref[...] = jnp.full(m_ref.shape, NEG_BIG, jnp.float32)
        acc_ref[...] = jnp.zeros(acc_ref.shape, jnp.float32)

        def step(t, buf):
            consume(qi, qi - t, buf)

        def issue(t, buf):
            produce(qs_ref, qi, qi - t, buf, False)

        def produce_next():
            produce(qsn_ref, qnext, qnext, buf_c, True)

        score_bound = _dot(kabs8, jnp.abs(qs_ref[:ATT_HEAD_DIM, :]))[:1, :]
        gap = jnp.max(score_bound - mxc_ref[...], axis=1, keepdims=True) + SKIP_LOG2
        need = jnp.floor(gap * inv_chunk_bias)[0, 0] + 1.0
        qf = jnp.asarray(qi, jnp.float32)
        nsteps = jnp.maximum(jnp.where(need < qf, need, qf), 1.0).astype(jnp.int32)

        produce(qs_ref, qi, jnp.maximum(qi - 1, 0), buf_a, False)
        step(0, buf_c)

        npair = (nsteps - 1) // 2

        def pair(i, c):
            t = 2 * i + 1
            issue(t + 1, buf_b)
            step(t, buf_a)
            issue(t + 2, buf_a)
            step(t + 1, buf_b)
            return c

        def two_pairs(i, c):
            return pair(2 * i + 1, pair(2 * i, c))

        lax.fori_loop(0, npair // 2, two_pairs, 0)
        lax.fori_loop(2 * (npair // 2), npair, pair, 0)
        t0 = 2 * npair + 1
        rest = nsteps - 2 * npair

        @pl.when(qi == 0)
        def _():
            produce_next()

        @pl.when(jnp.logical_and(qi >= 1, rest == 1))
        def _():
            produce_next()
            step(t0, buf_a)

        @pl.when(jnp.logical_and(qi >= 1, rest == 2))
        def _():
            issue(t0 + 1, buf_b)
            step(t0, buf_a)
            produce_next()
            step(t0 + 1, buf_b)

        acc = acc_ref[:ATT_HEAD_DIM, :]
        l = acc_ref[ATT_HEAD_DIM:ATT_HEAD_DIM + 1, :]
        oT = acc[:, :blk] / l[:, :blk] - lam * (acc[:, blk:] / l[:, blk:])
        ms = jnp.mean(oT * oT, axis=0, keepdims=True)
        oT = oT * lax.rsqrt(ms + LN_EPS) * gain_ref[...] * (1.0 - lam_init)
        o_ref[0, pl.ds(pl.multiple_of(qi * blk, blk), blk), :] = oT.T.astype(o_ref.dtype)
        return carry

    lax.fori_loop(0, nq, qblock, 0)


def _diff_attention(slopes2, qT, k, vT, lq1, lk1, lq2, lk2, gain_col, lam_init):
    bsz, nq, _, bq = qT.shape
    _, nk, _, bk = vT.shape
    assert bq == bk and nq == nk
    s = nq * bq
    vec = lambda bi, h: (0, 0)
    head = lambda bi, h: (bi, 0, h, 0)
    return pl.pallas_call(
        functools.partial(_attn_kernel, lam_init=lam_init),
        grid=(bsz, ATT_HEADS),
        in_specs=[
            pl.BlockSpec(memory_space=pltpu.SMEM),
            pl.BlockSpec((1, nq, ATT_HEAD_DIM, bq), head),
            pl.BlockSpec((1, s, ATT_HEAD_DIM), lambda bi, h: (bi, 0, h)),
            pl.BlockSpec((1, nk, ATT_HEAD_DIM, bk), head),
            pl.BlockSpec((1, QK_DIM), vec),
            pl.BlockSpec((1, QK_DIM), vec),
            pl.BlockSpec((1, QK_DIM), vec),
            pl.BlockSpec((1, QK_DIM), vec),
            pl.BlockSpec((ATT_HEAD_DIM, 1), vec),
        ],
        out_specs=pl.BlockSpec((1, s, ATT_HEAD_DIM), lambda bi, h: (bi, 0, h)),
        out_shape=jax.ShapeDtypeStruct((bsz, s, ATT_WIDTH), jnp.bfloat16),
        scratch_shapes=[
            pltpu.VMEM((2 * ATT_HEAD_DIM, 2 * bq), jnp.bfloat16),
            pltpu.VMEM((2 * ATT_HEAD_DIM, 2 * bq), jnp.bfloat16),
            pltpu.VMEM((bk, ATT_HEAD_DIM), jnp.bfloat16),
            pltpu.VMEM((1, 2 * bq), jnp.float32),
            pltpu.VMEM((ATT_HEAD_DIM + ATT_SUM_ROWS, 2 * bq), jnp.float32),
            pltpu.VMEM((bk, 2 * bq), jnp.float32),
            pltpu.VMEM((bk, 2 * bq), jnp.float32),
            pltpu.VMEM((bk, 2 * bq), jnp.float32),
            pltpu.VMEM((1, 2 * bq), jnp.float32),
            pltpu.VMEM((1, 2 * bq), jnp.float32),
            pltpu.VMEM((1, 2 * bq), jnp.float32),
        ],
        compiler_params=pltpu.CompilerParams(
            dimension_semantics=("arbitrary", "arbitrary"),
            vmem_limit_bytes=VMEM_LIMIT),
        name="diff_attn",
    )(slopes2, qT, k, vT, lq1, lk1, lq2, lk2, gain_col)


def _memkv_kernel(mem_ref, w_ref, kc_ref, vc_ref, wb_ref):
    @pl.when(pl.program_id(0) == 0)
    def _():
        wb_ref[...] = w_ref[...].astype(jnp.bfloat16)

    mb = mem_ref[0].astype(jnp.bfloat16)
    kc_ref[0] = _dot(mb, wb_ref[:, :D_MODEL]).astype(jnp.bfloat16)
    vc_ref[0] = _dot(mb, wb_ref[:, D_MODEL:]).astype(jnp.bfloat16)


def _mem_kv(mem, w_kv):
    bsz, m, d = mem.shape
    return pl.pallas_call(
        _memkv_kernel,
        grid=(bsz,),
        in_specs=[
            pl.BlockSpec((1, m, d), lambda bi: (bi, 0, 0)),
            pl.BlockSpec((d, 2 * d), lambda bi: (0, 0), pipeline_mode=pl.Buffered(1)),
        ],
        out_specs=[
            pl.BlockSpec((1, m, d), lambda bi: (bi, 0, 0)),
            pl.BlockSpec((1, m, d), lambda bi: (bi, 0, 0)),
        ],
        out_shape=[
            jax.ShapeDtypeStruct((bsz, m, d), jnp.bfloat16),
            jax.ShapeDtypeStruct((bsz, m, d), jnp.bfloat16),
        ],
        scratch_shapes=[pltpu.VMEM((d, 2 * d), jnp.bfloat16)],
        compiler_params=pltpu.CompilerParams(
            dimension_semantics=("arbitrary",), vmem_limit_bytes=VMEM_LIMIT),
        name="mem_kv",
    )(mem, w_kv)


def _conv_pieces(glu_ref, halo, cw_ref, cb_ref, ng_ref, nb_ref, xx_ref, act_ref):
    tm = POST_TM
    first = CONV_HALO - (CONV_KERNEL - 1)

    def fill():
        for g in range(CONV_GROUPS):
            gs = slice(g * CONV_GROUP_DIM, (g + 1) * CONV_GROUP_DIM)
            xx_ref[g, :CONV_HALO, :] = halo[:, gs]
            xx_ref[g, CONV_HALO:, :] = glu_ref[:, gs]

    def sweep(g, r0):
        gs = slice(g * CONV_GROUP_DIM, (g + 1) * CONV_GROUP_DIM)
        accs = [jnp.broadcast_to(cb_ref[:, gs], (CONV_ROWS, CONV_GROUP_DIM))] * CONV_CHUNKS
        for j in range(CONV_KERNEL):
            wj = jnp.broadcast_to(cw_ref[j:j + 1, gs], (CONV_ROWS, CONV_GROUP_DIM))
            for rc in range(CONV_CHUNKS):
                r = r0 + rc * CONV_ROWS + first + j
                accs[rc] = accs[rc] + wj * xx_ref[g, r:r + CONV_ROWS, :]
        for rc in range(CONV_CHUNKS):
            y = _layer_norm(accs[rc], ng_ref[:, gs], nb_ref[:, gs])
            y = y * (1.0 / (1.0 + jnp.exp(-y)))
            r = r0 + rc * CONV_ROWS
            act_ref[r:r + CONV_ROWS, gs] = y.astype(jnp.bfloat16)

    return [fill] + [functools.partial(sweep, g, r0) for g in range(CONV_GROUPS)
                     for r0 in range(0, tm, CONV_ROWS * CONV_CHUNKS)]


def _post_kernel(xn_ref, att_ref, glu0_ref, glun_ref, halo_ref, kc_ref, vc_ref,
                 cw_ref, cb_ref, ng_ref, nb_ref, wpw_ref, bpw_ref,
                 wo_ref, g1_ref, b1_ref, wq_ref, wom_ref, g2_ref, b2_ref,
                 wg_ref, wu_ref, wd_ref, g3_ref, b3_ref, o_ref,
                 ca_ref, c_ref, xx_ref, act_ref, *, tiles_per_batch):
    bf16 = jnp.bfloat16
    i = pl.program_id(0)

    def pointwise():
        return (_dot(act_ref[...], wpw_ref[...]) + bpw_ref[...]).astype(bf16)

    @pl.when(i == 0)
    def _():
        for piece in _conv_pieces(glu0_ref, jnp.zeros((CONV_HALO, CONV_WIDTH), jnp.float32),
                                  cw_ref, cb_ref, ng_ref, nb_ref, xx_ref, act_ref):
            piece()
        c_ref[...] = pointwise()

    mix = _dot(att_ref[...], wo_ref[:ATT_WIDTH, :]) + _dot(c_ref[...], wo_ref[ATT_WIDTH:, :])

    x1 = _layer_norm(DEEPNORM_ALPHA * xn_ref[...] + mix, g1_ref[...], b1_ref[...])

    halo = halo_ref[...]
    halo = jnp.where((i + 1) % tiles_per_batch == 0, jnp.zeros_like(halo), halo)
    for piece in _conv_pieces(glun_ref, halo, cw_ref, cb_ref, ng_ref, nb_ref, xx_ref, act_ref):
        piece()

    qscale = (MEM_HEAD_DIM ** -0.5) * LOG2E
    qm = (_dot(x1.astype(bf16), wq_ref[...]) * qscale).astype(bf16)
    for h in range(MEM_HEADS):
        hs = slice(h * MEM_HEAD_DIM, (h + 1) * MEM_HEAD_DIM)
        logits = _dot_nt(qm[:, hs], kc_ref[0, :, hs])
        p = jnp.exp2(logits - jnp.max(logits, axis=-1, keepdims=True))
        denom = jnp.sum(p, axis=-1, keepdims=True)
        ca = _dot(p.astype(bf16), vc_ref[0, :, hs]) / denom
        ca_ref[:, hs] = ca.astype(bf16)
    x2 = _layer_norm(DEEPNORM_ALPHA * x1 + _dot(ca_ref[...], wom_ref[...]),
                     g2_ref[...], b2_ref[...])

    x2b = x2.astype(bf16)
    gate = _dot(x2b, wg_ref[...])
    up = _dot(x2b, wu_ref[...])
    hdn = (gate * (1.0 / (1.0 + jnp.exp(-gate))) * up).astype(bf16)
    o_ref[...] = _layer_norm(DEEPNORM_ALPHA * x2 + _dot(hdn, wd_ref[...]), g3_ref[...], b3_ref[...])
    c_ref[...] = pointwise()


def _post(xn, att, glu, kc, vc, cw, cb, ng, nb, wpw, bpw,
          wo, g1, b1, wq, wom, g2, b2, wg, wu, wd, g3, b3, tiles_per_batch):
    t, d = xn.shape
    nt = t // POST_TM
    cw_ = CONV_WIDTH
    halo_blocks = POST_TM // CONV_HALO
    row = lambda i: (i, 0)
    const = lambda i: (0, 0)
    nxt = lambda i: (jnp.minimum(i + 1, nt - 1), 0)
    nxt_halo = lambda i: (jnp.minimum(i + 1, nt - 1) * halo_blocks - 1, 0)

    def resident(shape):
        return pl.BlockSpec(shape, const, pipeline_mode=pl.Buffered(1))

    return pl.pallas_call(
        functools.partial(_post_kernel, tiles_per_batch=tiles_per_batch),
        grid=(nt,),
        in_specs=[
            pl.BlockSpec((POST_TM, d), row),
            pl.BlockSpec((POST_TM, ATT_WIDTH), row),
            resident((POST_TM, cw_)),
            pl.BlockSpec((POST_TM, cw_), nxt),
            pl.BlockSpec((CONV_HALO, cw_), nxt_halo),
            pl.BlockSpec((1, MEM_LEN, d), lambda i: (i // tiles_per_batch, 0, 0)),
            pl.BlockSpec((1, MEM_LEN, d), lambda i: (i // tiles_per_batch, 0, 0)),
            resident((CONV_KERNEL, cw_)), resident((1, cw_)), resident((1, cw_)), resident((1, cw_)),
            resident((cw_, cw_)), resident((1, cw_)),
            resident((d, d)), resident((1, d)), resident((1, d)),
            resident((d, d)), resident((d, d)), resident((1, d)), resident((1, d)),
            resident((d, D_FF)), resident((d, D_FF)), resident((D_FF, d)),
            resident((1, d)), resident((1, d)),
        ],
        out_specs=pl.BlockSpec((POST_TM, d), row),
        out_shape=jax.ShapeDtypeStruct((t, d), jnp.float32),
        scratch_shapes=[
            pltpu.VMEM((POST_TM, d), jnp.bfloat16),
            pltpu.VMEM((POST_TM, cw_), jnp.bfloat16),
            pltpu.VMEM((CONV_GROUPS, CONV_HALO + POST_TM, CONV_GROUP_DIM), jnp.float32),
            pltpu.VMEM((POST_TM, cw_), jnp.bfloat16),
        ],
        compiler_params=pltpu.CompilerParams(
            dimension_semantics=("arbitrary",), vmem_limit_bytes=VMEM_LIMIT),
        name="post",
    )(xn, att, glu, glu, glu, kc, vc, cw, cb, ng, nb, wpw, bpw,
      wo, g1, b1, wq, wom, g2, b2, wg, wu, wd, g3, b3)


def kernel(x, mem, in_norm_g, in_norm_b, w_in, lambda_q1, lambda_k1, lambda_q2, lambda_k2,
           subln_g, conv_w, conv_b, conv_norm_g, conv_norm_b, w_pw, b_pw, w_o, ln1_g, ln1_b,
           w_q_mem, w_kv_mem, w_o_mem, ln2_g, ln2_b, w_gate, w_up, w_down, ln3_g, ln3_b):
    assert DEPTH == 1 and w_in.shape[0] == 1
    bsz, s, d = x.shape
    bf16 = jnp.bfloat16
    row = lambda v: v.reshape(1, -1)
    l = 0
    lam_init = 0.8 - 0.6 * math.exp(-0.3 * l)
    slopes2 = jnp.asarray([_alibi_slope(h) * LOG2E for h in range(ATT_HEADS)], jnp.float32)
    xn, qT, k, vT, glu = _in_proj(x, row(in_norm_g), row(in_norm_b), w_in[l])
    att = _diff_attention(slopes2, qT, k, vT, row(lambda_q1[l]), row(lambda_k1[l]),
                          row(lambda_q2[l]), row(lambda_k2[l]),
                          subln_g[l].reshape(-1, 1), lam_init)
    kc, vc = _mem_kv(mem, w_kv_mem[l])
    out = _post(xn.reshape(bsz * s, d), att.reshape(bsz * s, ATT_WIDTH),
                glu.reshape(bsz * s, CONV_WIDTH), kc, vc,
                conv_w[l], row(conv_b[l]), row(conv_norm_g[l]), row(conv_norm_b[l]),
                w_pw[l].astype(bf16), row(b_pw[l]),
                w_o[l].astype(bf16), row(ln1_g[l]), row(ln1_b[l]),
                w_q_mem[l].astype(bf16), w_o_mem[l].astype(bf16), row(ln2_g[l]), row(ln2_b[l]),
                w_gate[l].astype(bf16), w_up[l].astype(bf16), w_down[l].astype(bf16),
                row(ln3_g[l]), row(ln3_b[l]), s // POST_TM)
    return out.reshape(bsz, s, d)
```

```python
import functools
import math

import jax
import jax.numpy as jnp
from jax import lax
from jax.experimental import pallas as pl
from jax.experimental.pallas import tpu as pltpu

D_MODEL = 1024
DEPTH = 1
MEM_LEN = 256
ATT_WIDTH = 512
CONV_WIDTH = 512
ATT_HEADS = 4
ATT_HEAD_DIM = 128
QK_DIM = 64
QK_TOTAL = 512
CONV_GROUPS = 4
CONV_GROUP_DIM = 128
CONV_KERNEL = 31
MEM_HEADS = 4
MEM_HEAD_DIM = 256
D_FF = 2816
LN_EPS = 1e-5
DEEPNORM_ALPHA = (2 * DEPTH) ** 0.25

LOG2E = 1.4426950408889634
NEG_BIG = -1e30
SKIP_LOG2 = 152.0

ATT_BQ = 512
ATT_BK = 512
ATT_SUM_ROWS = 16
PROJ_TM = 512
CONV_HALO = 32
CONV_ROWS = 64
CONV_CHUNKS = 2
POST_TM = 512

VMEM_LIMIT = 60 * 1024 * 1024

_NT = (((1,), (1,)), ((), ()))


def _layer_norm(x, g, b):
    mu = jnp.mean(x, axis=-1, keepdims=True)
    xc = x - mu
    var = jnp.mean(xc * xc, axis=-1, keepdims=True)
    return xc * lax.rsqrt(var + LN_EPS) * g + b


def _alibi_slope(h):
    return 2.0 ** (-8.0 * (h + 1) / ATT_HEADS)


def _dot(a, b):
    return jnp.dot(a, b, preferred_element_type=jnp.float32)


def _dot_nt(a, b):
    return lax.dot_general(a, b, _NT, preferred_element_type=jnp.float32)


def _inproj_kernel(x_ref, g_ref, b_ref, w_ref,
                   xn_ref, qT_ref, k_ref, vT_ref, glu_ref, wb_ref):
    tm = PROJ_TM
    half = tm // 2
    qscale = (QK_DIM ** -0.5) * LOG2E

    @pl.when(jnp.logical_and(pl.program_id(0) == 0, pl.program_id(1) == 0))
    def _():
        wb_ref[...] = w_ref[...].astype(jnp.bfloat16)

    wq_ref = wb_ref.at[:, :QK_TOTAL]
    wk_ref = wb_ref.at[:, QK_TOTAL:2 * QK_TOTAL]
    wv_ref = wb_ref.at[:, 2 * QK_TOTAL:2 * QK_TOTAL + ATT_WIDTH]
    wc_ref = wb_ref.at[:, 2 * QK_TOTAL + ATT_WIDTH:]
    for r in (slice(0, half), slice(half, tm)):
        xn = _layer_norm(x_ref[0, r, :], g_ref[...], b_ref[...])
        xn_ref[0, r, :] = xn
        xb = xn.astype(jnp.bfloat16)
        k_ref[0, r, :] = _dot(xb, wk_ref[...]).astype(jnp.bfloat16)
        qT_ref[0, 0, :, r] = (_dot(xb, wq_ref[...]) * qscale).T.astype(jnp.bfloat16)
        vT_ref[0, 0, :, r] = _dot(xb, wv_ref[...]).T.astype(jnp.bfloat16)
        c_val = _dot(xb, wc_ref[:, :CONV_WIDTH])
        c_gate = _dot(xb, wc_ref[:, CONV_WIDTH:])
        glu_ref[0, r, :] = c_val * (1.0 / (1.0 + jnp.exp(-c_gate)))


def _in_proj(x, g, b, w):
    assert PROJ_TM == ATT_BQ == ATT_BK
    bsz, s, d = x.shape
    assert s % PROJ_TM == 0 and d == D_MODEL
    nt = s // PROJ_TM
    nbq = PROJ_TM // ATT_BQ
    nbk = PROJ_TM // ATT_BK
    const = lambda bi, si: (0, 0)
    return pl.pallas_call(
        _inproj_kernel,
        grid=(bsz, nt),
        in_specs=[
            pl.BlockSpec((1, PROJ_TM, d), lambda bi, si: (bi, si, 0)),
            pl.BlockSpec((1, d), const),
            pl.BlockSpec((1, d), const),
            pl.BlockSpec(w.shape, const, pipeline_mode=pl.Buffered(1)),
        ],
        out_specs=[
            pl.BlockSpec((1, PROJ_TM, d), lambda bi, si: (bi, si, 0)),
            pl.BlockSpec((1, nbq, QK_TOTAL, ATT_BQ), lambda bi, si: (bi, si, 0, 0)),
            pl.BlockSpec((1, PROJ_TM, QK_TOTAL), lambda bi, si: (bi, si, 0)),
            pl.BlockSpec((1, nbk, ATT_WIDTH, ATT_BK), lambda bi, si: (bi, si, 0, 0)),
            pl.BlockSpec((1, PROJ_TM, CONV_WIDTH), lambda bi, si: (bi, si, 0)),
        ],
        out_shape=[
            jax.ShapeDtypeStruct((bsz, s, d), jnp.float32),
            jax.ShapeDtypeStruct((bsz, s // ATT_BQ, QK_TOTAL, ATT_BQ), jnp.bfloat16),
            jax.ShapeDtypeStruct((bsz, s, QK_TOTAL), jnp.bfloat16),
            jax.ShapeDtypeStruct((bsz, s // ATT_BK, ATT_WIDTH, ATT_BK), jnp.bfloat16),
            jax.ShapeDtypeStruct((bsz, s, CONV_WIDTH), jnp.float32),
        ],
        scratch_shapes=[pltpu.VMEM(w.shape, jnp.bfloat16)],
        compiler_params=pltpu.CompilerParams(
            dimension_semantics=("arbitrary", "arbitrary"),
            vmem_limit_bytes=VMEM_LIMIT),
        name="in_proj",
    )(x, g, b, w)


def _attn_kernel(slope_ref, qT_ref, k_ref, vT_ref, lq1_ref, lk1_ref, lq2_ref, lk2_ref,
                 gain_ref, o_ref, qs_ref, qsn_ref, kbias_ref, m_ref, acc_ref,
                 sa_ref, sb_ref, sc_ref, mxa_ref, mxb_ref, mxc_ref, *, lam_init):
    blk = ATT_BQ
    nq = qT_ref.shape[1]
    slope2 = slope_ref[pl.program_id(1)]

    pos = lax.broadcasted_iota(jnp.int32, (blk, ATT_HEAD_DIM), 0)
    lane = lax.broadcasted_iota(jnp.int32, (blk, ATT_HEAD_DIM), 1)
    u = slope2 * pos.astype(jnp.float32)
    hi = u.astype(jnp.bfloat16).astype(jnp.float32)
    mid = (u - hi).astype(jnp.bfloat16).astype(jnp.float32)
    lo = u - hi - mid
    cols = jnp.where(lane == 0, hi, jnp.where(lane == 1, mid, jnp.where(lane == 2, lo, 0.0)))
    kbias_ref[...] = cols.astype(jnp.bfloat16)

    row2 = lax.broadcasted_iota(jnp.int32, (ATT_HEAD_DIM, 2 * blk), 0)
    ones3 = jnp.where(row2 < 3, 1.0, 0.0).astype(jnp.bfloat16)
    qs_ref[ATT_HEAD_DIM:, :] = ones3
    qsn_ref[ATT_HEAD_DIM:, :] = ones3

    def stack_q(dst_ref, qidx):
        qT = qT_ref[0, qidx]
        row = lax.broadcasted_iota(jnp.int32, (ATT_HEAD_DIM, blk), 0)
        zero = jnp.zeros_like(qT)
        dst_ref[:ATT_HEAD_DIM, :blk] = jnp.where(row < QK_DIM, qT, zero)
        dst_ref[:ATT_HEAD_DIM, blk:] = jnp.where(row >= QK_DIM, qT, zero)

    ones_rows = jnp.ones((ATT_SUM_ROWS, blk), jnp.bfloat16)

    def produce(q_ref, qi, j, buf, masked):
        s_ref, mx_ref = buf
        start = j * blk if isinstance(j, int) else pl.multiple_of(j * blk, blk)
        kb = jnp.concatenate([k_ref[0, pl.ds(start, blk), :], kbias_ref[...]], axis=1)
        s = _dot(kb, q_ref[...])
        if masked:
            kr = lax.broadcasted_iota(jnp.int32, (blk, blk), 0)
            qc = lax.broadcasted_iota(jnp.int32, (blk, blk), 1)
            valid = (kr - qc) <= (qi - j) * blk
            s = jnp.where(jnp.concatenate([valid, valid], axis=1), s, NEG_BIG)
        s_ref[...] = s
        mx_ref[...] = jnp.max(s, axis=0, keepdims=True)

    def consume(qi, j, buf):
        s_ref, mx_ref = buf
        off = slope2 * jnp.asarray((j - qi) * blk, jnp.float32)
        m_old = m_ref[...]
        m_new = jnp.maximum(m_old, mx_ref[...] + off)
        p = jnp.exp2(s_ref[...] - (m_new - off)).astype(jnp.bfloat16)
        alpha = jnp.exp2(m_old - m_new)
        lhs = jnp.concatenate([vT_ref[0, j], ones_rows], axis=0)
        acc_ref[...] = alpha * acc_ref[...] + _dot(lhs, p)
        m_ref[...] = m_new

    lam = (jnp.exp(jnp.sum(lq1_ref[...] * lk1_ref[...], axis=-1, keepdims=True))
           - jnp.exp(jnp.sum(lq2_ref[...] * lk2_ref[...], axis=-1, keepdims=True))
           + lam_init)

    kabs = jnp.max(jnp.abs(k_ref[0].astype(jnp.float32)), axis=0, keepdims=True)
    row8 = lax.broadcasted_iota(jnp.int32, (8, ATT_HEAD_DIM), 0)
    kabs8 = jnp.where(row8 == 0, kabs, 0.0).astype(jnp.bfloat16)
    inv_chunk_bias = 1.0 / (slope2 * blk)

    buf_a, buf_b, buf_c = (sa_ref, mxa_ref), (sb_ref, mxb_ref), (sc_ref, mxc_ref)

    stack_q(qsn_ref, 0)
    produce(qsn_ref, 0, 0, buf_c, True)

    def qblock(qi, carry):
        qnext = jnp.minimum(qi + 1, nq - 1)
        stack_q(qs_ref, qi)
        stack_q(qsn_ref, qnext)
        m_ref[...] = jnp.full(m_ref.shape, NEG_BIG, jnp.float32)
        acc_ref[...] = jnp.zeros(acc_ref.shape, jnp.float32)

        def step(t, buf):
            consume(qi, qi - t, buf)

        def issue(t, buf):
            produce(qs_ref, qi, qi - t, buf, False)

        def produce_next():
            produce(qsn_ref, qnext, qnext, buf_c, True)

        score_bound = _dot(kabs8, jnp.abs(qs_ref[:ATT_HEAD_DIM, :]))[:1, :]
        gap = jnp.max(score_bound - mxc_ref[...], axis=1, keepdims=True) + SKIP_LOG2
        need = jnp.floor(gap * inv_chunk_bias)[0, 0] + 1.0
        qf = jnp.asarray(qi, jnp.float32)
        nsteps = jnp.maximum(jnp.where(need < qf, need, qf), 1.0).astype(jnp.int32)

        produce(qs_ref, qi, jnp.maximum(qi - 1, 0), buf_a, False)
        step(0, buf_c)

        npair = (nsteps - 1) // 2

        def pair(i, c):
            t = 2 * i + 1
            issue(t + 1, buf_b)
            step(t, buf_a)
            issue(t + 2, buf_a)
            step(t + 1, buf_b)
            return c

        def two_pairs(i, c):
            return pair(2 * i + 1, pair(2 * i, c))

        lax.fori_loop(0, npair // 2, two_pairs, 0)
        lax.fori_loop(2 * (npair // 2), npair, pair, 0)
        t0 = 2 * npair + 1
        rest = nsteps - 2 * npair

        @pl.when(qi == 0)
        def _():
            produce_next()

        @pl.when(jnp.logical_and(qi >= 1, rest == 1))
        def _():
            produce_next()
            step(t0, buf_a)

        @pl.when(jnp.logical_and(qi >= 1, rest == 2))
        def _():
            issue(t0 + 1, buf_b)
            step(t0, buf_a)
            produce_next()
            step(t0 + 1, buf_b)

        acc = acc_ref[:ATT_HEAD_DIM, :]
        l = acc_ref[ATT_HEAD_DIM:ATT_HEAD_DIM + 1, :]
        oT = acc[:, :blk] / l[:, :blk] - lam * (acc[:, blk:] / l[:, blk:])
        ms = jnp.mean(oT * oT, axis=0, keepdims=True)
        oT = oT * lax.rsqrt(ms + LN_EPS) * gain_ref[...] * (1.0 - lam_init)
        o_ref[0, pl.ds(pl.multiple_of(qi * blk, blk), blk), :] = oT.T.astype(o_ref.dtype)
        return carry

    lax.fori_loop(0, nq, qblock, 0)


def _diff_attention(slopes2, qT, k, vT, lq1, lk1, lq2, lk2, gain_col, lam_init):
    bsz, nq, _, bq = qT.shape
    _, nk, _, bk = vT.shape
    assert bq == bk and nq == nk
    s = nq * bq
    vec = lambda bi, h: (0, 0)
    head = lambda bi, h: (bi, 0, h, 0)
    return pl.pallas_call(
        functools.partial(_attn_kernel, lam_init=lam_init),
        grid=(bsz, ATT_HEADS),
        in_specs=[
            pl.BlockSpec(memory_space=pltpu.SMEM),
            pl.BlockSpec((1, nq, ATT_HEAD_DIM, bq), head),
            pl.BlockSpec((1, s, ATT_HEAD_DIM), lambda bi, h: (bi, 0, h)),
            pl.BlockSpec((1, nk, ATT_HEAD_DIM, bk), head),
            pl.BlockSpec((1, QK_DIM), vec),
            pl.BlockSpec((1, QK_DIM), vec),
            pl.BlockSpec((1, QK_DIM), vec),
            pl.BlockSpec((1, QK_DIM), vec),
            pl.BlockSpec((ATT_HEAD_DIM, 1), vec),
        ],
        out_specs=pl.BlockSpec((1, s, ATT_HEAD_DIM), lambda bi, h: (bi, 0, h)),
        out_shape=jax.ShapeDtypeStruct((bsz, s, ATT_WIDTH), jnp.bfloat16),
        scratch_shapes=[
            pltpu.VMEM((2 * ATT_HEAD_DIM, 2 * bq), jnp.bfloat16),
            pltpu.VMEM((2 * ATT_HEAD_DIM, 2 * bq), jnp.bfloat16),
            pltpu.VMEM((bk, ATT_HEAD_DIM), jnp.bfloat16),
            pltpu.VMEM((1, 2 * bq), jnp.float32),
            pltpu.VMEM((ATT_HEAD_DIM + ATT_SUM_ROWS, 2 * bq), jnp.float32),
            pltpu.VMEM((bk, 2 * bq), jnp.float32),
            pltpu.VMEM((bk, 2 * bq), jnp.float32),
            pltpu.VMEM((bk, 2 * bq), jnp.float32),
            pltpu.VMEM((1, 2 * bq), jnp.float32),
            pltpu.VMEM((1, 2 * bq), jnp.float32),
            pltpu.VMEM((1, 2 * bq), jnp.float32),
        ],
        compiler_params=pltpu.CompilerParams(
            dimension_semantics=("arbitrary", "arbitrary"),
            vmem_limit_bytes=VMEM_LIMIT),
        name="diff_attn",
    )(slopes2, qT, k, vT, lq1, lk1, lq2, lk2, gain_col)


def _memkv_kernel(mem_ref, w_ref, kc_ref, vc_ref, wb_ref):
    @pl.when(pl.program_id(0) == 0)
    def _():
        wb_ref[...] = w_ref[...].astype(jnp.bfloat16)

    mb = mem_ref[0].astype(jnp.bfloat16)
    kc_ref[0] = _dot(mb, wb_ref[:, :D_MODEL]).astype(jnp.bfloat16)
    vc_ref[0] = _dot(mb, wb_ref[:, D_MODEL:]).astype(jnp.bfloat16)


def _mem_kv(mem, w_kv):
    bsz, m, d = mem.shape
    return pl.pallas_call(
        _memkv_kernel,
        grid=(bsz,),
        in_specs=[
            pl.BlockSpec((1, m, d), lambda bi: (bi, 0, 0)),
            pl.BlockSpec((d, 2 * d), lambda bi: (0, 0), pipeline_mode=pl.Buffered(1)),
        ],
        out_specs=[
            pl.BlockSpec((1, m, d), lambda bi: (bi, 0, 0)),
            pl.BlockSpec((1, m, d), lambda bi: (bi, 0, 0)),
        ],
        out_shape=[
            jax.ShapeDtypeStruct((bsz, m, d), jnp.bfloat16),
            jax.ShapeDtypeStruct((bsz, m, d), jnp.bfloat16),
        ],
        scratch_shapes=[pltpu.VMEM((d, 2 * d), jnp.bfloat16)],
        compiler_params=pltpu.CompilerParams(
            dimension_semantics=("arbitrary",), vmem_limit_bytes=VMEM_LIMIT),
        name="mem_kv",
    )(mem, w_kv)


def _conv_pieces(glu_ref, halo, cw_ref, cb_ref, ng_ref, nb_ref, xx_ref, act_ref):
    tm = POST_TM
    first = CONV_HALO - (CONV_KERNEL - 1)

    def fill():
        for g in range(CONV_GROUPS):
            gs = slice(g * CONV_GROUP_DIM, (g + 1) * CONV_GROUP_DIM)
            xx_ref[g, :CONV_HALO, :] = halo[:, gs]
            xx_ref[g, CONV_HALO:, :] = glu_ref[:, gs]

    def sweep(g, r0):
        gs = slice(g * CONV_GROUP_DIM, (g + 1) * CONV_GROUP_DIM)
        accs = [jnp.broadcast_to(cb_ref[:, gs], (CONV_ROWS, CONV_GROUP_DIM))] * CONV_CHUNKS
        for j in range(CONV_KERNEL):
            wj = jnp.broadcast_to(cw_ref[j:j + 1, gs], (CONV_ROWS, CONV_GROUP_DIM))
            for rc in range(CONV_CHUNKS):
                r = r0 + rc * CONV_ROWS + first + j
                accs[rc] = accs[rc] + wj * xx_ref[g, r:r + CONV_ROWS, :]
        for rc in range(CONV_CHUNKS):
            y = _layer_norm(accs[rc], ng_ref[:, gs], nb_ref[:, gs])
            y = y * (1.0 / (1.0 + jnp.exp(-y)))
            r = r0 + rc * CONV_ROWS
            act_ref[r:r + CONV_ROWS, gs] = y.astype(jnp.bfloat16)

    return [fill] + [functools.partial(sweep, g, r0) for g in range(CONV_GROUPS)
                     for r0 in range(0, tm, CONV_ROWS * CONV_CHUNKS)]


def _post_kernel(xn_ref, att_ref, glu0_ref, glun_ref, halo_ref, kc_ref, vc_ref,
                 cw_ref, cb_ref, ng_ref, nb_ref, wpw_ref, bpw_ref,
                 wo_ref, g1_ref, b1_ref, wq_ref, wom_ref, g2_ref, b2_ref,
                 wg_ref, wu_ref, wd_ref, g3_ref, b3_ref, o_ref,
                 ca_ref, c_ref, xx_ref, act_ref, *, tiles_per_batch):
    bf16 = jnp.bfloat16
    i = pl.program_id(0)

    def pointwise():
        return (_dot(act_ref[...], wpw_ref[...]) + bpw_ref[...]).astype(bf16)

    @pl.when(i == 0)
    def _():
        for piece in _conv_pieces(glu0_ref, jnp.zeros((CONV_HALO, CONV_WIDTH), jnp.float32),
                                  cw_ref, cb_ref, ng_ref, nb_ref, xx_ref, act_ref):
            piece()
        c_ref[...] = pointwise()

    mix = _dot(att_ref[...], wo_ref[:ATT_WIDTH, :]) + _dot(c_ref[...], wo_ref[ATT_WIDTH:, :])

    x1 = _layer_norm(DEEPNORM_ALPHA * xn_ref[...] + mix, g1_ref[...], b1_ref[...])

    halo = halo_ref[...]
    halo = jnp.where((i + 1) % tiles_per_batch == 0, jnp.zeros_like(halo), halo)
    for piece in _conv_pieces(glun_ref, halo, cw_ref, cb_ref, ng_ref, nb_ref, xx_ref, act_ref):
        piece()

    qscale = (MEM_HEAD_DIM ** -0.5) * LOG2E
    qm = (_dot(x1.astype(bf16), wq_ref[...]) * qscale).astype(bf16)
    for h in range(MEM_HEADS):
        hs = slice(h * MEM_HEAD_DIM, (h + 1) * MEM_HEAD_DIM)
        logits = _dot_nt(qm[:, hs], kc_ref[0, :, hs])
        p = jnp.exp2(logits - jnp.max(logits, axis=-1, keepdims=True))
        denom = jnp.sum(p, axis=-1, keepdims=True)
        ca = _dot(p.astype(bf16), vc_ref[0, :, hs]) / denom
        ca_ref[:, hs] = ca.astype(bf16)
    x2 = _layer_norm(DEEPNORM_ALPHA * x1 + _dot(ca_ref[...], wom_ref[...]),
                     g2_ref[...], b2_ref[...])

    x2b = x2.astype(bf16)
    gate = _dot(x2b, wg_ref[...])
    up = _dot(x2b, wu_ref[...])
    hdn = (gate * (1.0 / (1.0 + jnp.exp(-gate))) * up).astype(bf16)
    o_ref[...] = _layer_norm(DEEPNORM_ALPHA * x2 + _dot(hdn, wd_ref[...]), g3_ref[...], b3_ref[...])
    c_ref[...] = pointwise()


def _post(xn, att, glu, kc, vc, cw, cb, ng, nb, wpw, bpw,
          wo, g1, b1, wq, wom, g2, b2, wg, wu, wd, g3, b3, tiles_per_batch):
    t, d = xn.shape
    nt = t // POST_TM
    cw_ = CONV_WIDTH
    halo_blocks = POST_TM // CONV_HALO
    row = lambda i: (i, 0)
    const = lambda i: (0, 0)
    nxt = lambda i: (jnp.minimum(i + 1, nt - 1), 0)
    nxt_halo = lambda i: (jnp.minimum(i + 1, nt - 1) * halo_blocks - 1, 0)

    def resident(shape):
        return pl.BlockSpec(shape, const, pipeline_mode=pl.Buffered(1))

    return pl.pallas_call(
        functools.partial(_post_kernel, tiles_per_batch=tiles_per_batch),
        grid=(nt,),
        in_specs=[
            pl.BlockSpec((POST_TM, d), row),
            pl.BlockSpec((POST_TM, ATT_WIDTH), row),
            resident((POST_TM, cw_)),
            pl.BlockSpec((POST_TM, cw_), nxt),
            pl.BlockSpec((CONV_HALO, cw_), nxt_halo),
            pl.BlockSpec((1, MEM_LEN, d), lambda i: (i // tiles_per_batch, 0, 0)),
            pl.BlockSpec((1, MEM_LEN, d), lambda i: (i // tiles_per_batch, 0, 0)),
            resident((CONV_KERNEL, cw_)), resident((1, cw_)), resident((1, cw_)), resident((1, cw_)),
            resident((cw_, cw_)), resident((1, cw_)),
            resident((d, d)), resident((1, d)), resident((1, d)),
            resident((d, d)), resident((d, d)), resident((1, d)), resident((1, d)),
            resident((d, D_FF)), resident((d, D_FF)), resident((D_FF, d)),
            resident((1, d)), resident((1, d)),
        ],
        out_specs=pl.BlockSpec((POST_TM, d), row),
        out_shape=jax.ShapeDtypeStruct((t, d), jnp.float32),
        scratch_shapes=[
            pltpu.VMEM((POST_TM, d), jnp.bfloat16),
            pltpu.VMEM((POST_TM, cw_), jnp.bfloat16),
            pltpu.VMEM((CONV_GROUPS, CONV_HALO + POST_TM, CONV_GROUP_DIM), jnp.float32),
            pltpu.VMEM((POST_TM, cw_), jnp.bfloat16),
        ],
        compiler_params=pltpu.CompilerParams(
            dimension_semantics=("arbitrary",), vmem_limit_bytes=VMEM_LIMIT),
        name="post",
    )(xn, att, glu, glu, glu, kc, vc, cw, cb, ng, nb, wpw, bpw,
      wo, g1, b1, wq, wom, g2, b2, wg, wu, wd, g3, b3)


def kernel(x, mem, in_norm_g, in_norm_b, w_in, lambda_q1, lambda_k1, lambda_q2, lambda_k2,
           subln_g, conv_w, conv_b, conv_norm_g, conv_norm_b, w_pw, b_pw, w_o, ln1_g, ln1_b,
           w_q_mem, w_kv_mem, w_o_mem, ln2_g, ln2_b, w_gate, w_up, w_down, ln3_g, ln3_b):
    assert DEPTH == 1 and w_in.shape[0] == 1
    bsz, s, d = x.shape
    bf16 = jnp.bfloat16
    row = lambda v: v.reshape(1, -1)
    l = 0
    lam_init = 0.8 - 0.6 * math.exp(-0.3 * l)
    slopes2 = jnp.asarray([_alibi_slope(h) * LOG2E for h in range(ATT_HEADS)], jnp.float32)
    xn, qT, k, vT, glu = _in_proj(x, row(in_norm_g), row(in_norm_b), w_in[l])
    att = _diff_attention(slopes2, qT, k, vT, row(lambda_q1[l]), row(lambda_k1[l]),
                          row(lambda_q2[l]), row(lambda_k2[l]),
                          subln_g[l].reshape(-1, 1), lam_init)
    kc, vc = _mem_kv(mem, w_kv_mem[l])
    out = _post(xn.reshape(bsz * s, d), att.reshape(bsz * s, ATT_WIDTH),
                glu.reshape(bsz * s, CONV_WIDTH), kc, vc,
                conv_w[l], row(conv_b[l]), row(conv_norm_g[l]), row(conv_norm_b[l]),
                w_pw[l].astype(bf16), row(b_pw[l]),
                w_o[l].astype(bf16), row(ln1_g[l]), row(ln1_b[l]),
                w_q_mem[l].astype(bf16), w_o_mem[l].astype(bf16), row(ln2_g[l]), row(ln2_b[l]),
                w_gate[l].astype(bf16), w_up[l].astype(bf16), w_down[l].astype(bf16),
                row(ln3_g[l]), row(ln3_b[l]), s // POST_TM)
    return out.reshape(bsz, s, d)
```
